```python
import math
import jax, jax.numpy as jnp
from jax import lax
import numpy as np

D_MODEL = 2048
BATCH = 8
SEQ = 2048
DEPTH = 2
DEC_BATCH = 4
DEC_SEQ = 8192
PAST_LEN = 128

ML_HEADS = 4
ML_HEAD_DIM = 256
ML_WIDTH = ML_HEADS * ML_HEAD_DIM
ML_CHUNK = 128
F_BIAS_LO = 3.0
F_BIAS_HI = 6.0
HY_WIDTH = 1024
HY_ORDER = 2
HY_BANDS = 16
HY_EMB = 2 * HY_BANDS + 1
HY_FFN = 64
HY_DECAY_MIN = 3.0
HY_DECAY_MAX = 15.0
MLA_HEADS = 8
MLA_Q_RANK = 512
MLA_KV_RANK = 256
MLA_NOPE = 128
MLA_ROPE = 64
MLA_V = 128
MLA_WIDTH = MLA_HEADS * MLA_V
ROPE_BASE = 10000.0
Q_BLOCK = 128
N_BRANCH = 3
N_EXPERTS = 16
N_GROUPS = 4
EXPERTS_PER_GROUP = N_EXPERTS // N_GROUPS
TOP_K = 2
D_EXPERT = 1024
EPS = 1e-6

IN_SIZES = (ML_WIDTH, ML_WIDTH, ML_WIDTH, ML_WIDTH, 4 * ML_HEADS, (HY_ORDER + 1) * HY_WIDTH, MLA_Q_RANK, MLA_KV_RANK, MLA_ROPE, N_BRANCH * D_MODEL)
IN_COLS = sum(IN_SIZES)

kernel_name = 'hybrid_bidir_mlstm_hyena_mla_moe_encoder'


def rmsnorm(x, g):
    xf = x.astype(jnp.float32)
    y = xf * lax.rsqrt(jnp.mean(xf * xf, axis=-1, keepdims=True) + EPS)
    return (y * g.astype(jnp.float32)).astype(x.dtype)


def mlstm_dir(q, k, v, i_pre, f_pre):
    f32 = jnp.float32
    B, H, L, dk = q.shape
    dv = v.shape[-1]
    nc = L // ML_CHUNK

    def to_chunks(a):
        return jnp.moveaxis(a.reshape(B, H, nc, ML_CHUNK, *a.shape[3:]), 2, 0)

    logf = jax.nn.log_sigmoid(f_pre.astype(f32))
    logi = i_pre.astype(f32)
    causal = jnp.tril(jnp.ones((ML_CHUNK, ML_CHUNK), dtype=bool))

    def step(carry, inp):
        C, n, m = carry
        qj, kj, vj, lf, li = inp
        qf, kf, vf = qj.astype(f32), kj.astype(f32), vj.astype(f32)
        b = jnp.cumsum(lf, axis=-1)
        Dm = jnp.where(causal, b[..., :, None] - b[..., None, :] + li[..., None, :], -jnp.inf)
        inter = b + m[..., None]
        m_row = jnp.maximum(inter, jnp.max(Dm, axis=-1))
        w_inter = jnp.exp(inter - m_row)
        s = jnp.einsum('bhid,bhjd->bhij', qf, kf) * jnp.exp(Dm - m_row[..., None])
        num = w_inter[..., None] * jnp.einsum('bhid,bhde->bhie', qf, C) + jnp.einsum('bhij,bhje->bhie', s, vf)
        den = w_inter * jnp.einsum('bhid,bhd->bhi', qf, n) + jnp.sum(s, axis=-1)
        h = num / jnp.maximum(jnp.abs(den), jnp.exp(-m_row))[..., None]
        bL = b[..., -1]
        g = bL[..., None] - b + li
        m_new = jnp.maximum(bL + m, jnp.max(g, axis=-1))
        a = jnp.exp(bL + m - m_new)
        wk = jnp.exp(g - m_new[..., None])
        C_new = a[..., None, None] * C + jnp.einsum('bhj,bhjd,bhje->bhde', wk, kf, vf)
        n_new = a[..., None] * n + jnp.einsum('bhj,bhjd->bhd', wk, kf)
        return (C_new, n_new, m_new), h.astype(qj.dtype)

    init = (jnp.zeros((B, H, dk, dv), f32), jnp.zeros((B, H, dk), f32), jnp.zeros((B, H), f32))
    _, hs = lax.scan(step, init, (to_chunks(q), to_chunks(k), to_chunks(v), to_chunks(logf), to_chunks(logi)))
    return jnp.moveaxis(hs, 0, 2).reshape(B, H, L, dv)


def rope(x, cos, sin):
    half = MLA_ROPE // 2
    x1, x2 = x[..., :half], x[..., half:]
    return jnp.concatenate([x1 * cos - x2 * sin, x1 * sin + x2 * cos], axis=-1).astype(x.dtype)


def hyena_filters(L, fw1, fb1, fw2, fb2, fw3, freq, log_decay):
    f32 = jnp.float32
    t = jnp.arange(L, dtype=f32) / L
    bands = jnp.arange(1, HY_BANDS + 1, dtype=f32)
    ang = 2.0 * math.pi * t[:, None] * bands[None, :]
    feat = jnp.concatenate([t[:, None], jnp.cos(ang), jnp.sin(ang)], axis=-1)
    h = jnp.sin(freq[0].astype(f32) * (feat @ fw1.astype(f32) + fb1.astype(f32)))
    h = jnp.sin(freq[1].astype(f32) * (h @ fw2.astype(f32) + fb2.astype(f32)))
    h = h @ fw3.astype(f32)
    h = h * jnp.exp(-t[:, None] * jnp.exp(log_decay.astype(f32)))
    h = h.reshape(L, HY_ORDER, 2, HY_WIDTH)
    fwd, bwd = h[:, :, 0], h[:, :, 1]
    kfull = jnp.concatenate([fwd, jnp.zeros((1, HY_ORDER, HY_WIDTH), f32), jnp.flip(bwd[1:], axis=0)], axis=0)
    return kfull / jnp.sum(jnp.abs(kfull), axis=0, keepdims=True)


def long_conv(z, k_spec, skip):
    L = z.shape[1]
    zf = z.astype(jnp.float32)
    Z = jnp.fft.rfft(zf, n=2 * L, axis=1)
    y = jnp.fft.irfft(Z * k_spec[None], n=2 * L, axis=1)[:, :L]
    return (y + skip.astype(jnp.float32) * zf).astype(z.dtype)


def mixer(h, l, P):
    f32 = jnp.float32
    B, L, _ = h.shape
    offsets = [int(o) for o in np.cumsum(IN_SIZES)[:-1]]
    proj = h @ P['w_in'][l]
    ml_q, ml_k, ml_v, ml_o, ml_g, hy_u, cq, ckv, kr, br_g = jnp.split(proj, offsets, axis=-1)

    def heads(a):
        return a.reshape(B, L, ML_HEADS, ML_HEAD_DIM).transpose(0, 2, 1, 3)
    q = heads(ml_q)
    k = heads(ml_k) * (ML_HEAD_DIM ** -0.5)
    v = heads(ml_v)
    gates = (ml_g.reshape(B, L, 4, ML_HEADS).astype(f32) + P['ml_gate_b'][l].astype(f32)).transpose(0, 2, 3, 1)
    rev = lambda a: jnp.flip(a, axis=2)
    h_f = mlstm_dir(q, k, v, gates[:, 0], gates[:, 1])
    h_b = rev(mlstm_dir(rev(q), rev(k), rev(v), rev(gates[:, 2]), rev(gates[:, 3])))
    hs = (h_f + h_b).transpose(0, 2, 1, 3)
    hs = rmsnorm(hs, P['ml_norm_g'][l].reshape(ML_HEADS, ML_HEAD_DIM))
    o_ml = hs.reshape(B, L, ML_WIDTH) * jax.nn.sigmoid(ml_o)

    up = jnp.pad(hy_u, ((0, 0), (1, 1), (0, 0)))
    cw = P['hy_conv_w'][l]
    uc = up[:, :-2] * cw[0] + up[:, 1:-1] * cw[1] + up[:, 2:] * cw[2] + P['hy_conv_b'][l]
    hv, x1, x2 = jnp.split(uc, 3, axis=-1)
    kfull = hyena_filters(L, P['hy_fw1'][l], P['hy_fb1'][l], P['hy_fw2'][l], P['hy_fb2'][l],
                          P['hy_fw3'][l], P['hy_freq'][l], P['hy_log_decay'][l])
    skip = P['hy_skip'][l]
    z = x1 * long_conv(hv, jnp.fft.rfft(kfull[:, 0], axis=0), skip[0])
    o_hy = x2 * long_conv(z, jnp.fft.rfft(kfull[:, 1], axis=0), skip[1])

    cq = rmsnorm(cq, P['mla_q_norm_g'][l])
    qa = (cq @ P['w_uq'][l]).reshape(B, L, MLA_HEADS, MLA_NOPE + MLA_ROPE)
    qn, qr = qa[..., :MLA_NOPE], qa[..., MLA_NOPE:]
    ckv = rmsnorm(ckv, P['mla_kv_norm_g'][l])
    kv = (ckv @ P['w_ukv'][l]).reshape(B, L, MLA_HEADS, MLA_NOPE + MLA_V)
    kn, vv = kv[..., :MLA_NOPE], kv[..., MLA_NOPE:]
    pos = jnp.arange(L, dtype=f32)
    inv = ROPE_BASE ** (-jnp.arange(0, MLA_ROPE, 2, dtype=f32) / MLA_ROPE)
    ang = pos[:, None] * inv[None, :]
    cos, sin = jnp.cos(ang), jnp.sin(ang)
    qr = rope(qr, cos[:, None, :], sin[:, None, :])
    kr = rope(kr, cos, sin)
    nb = L // Q_BLOCK
    scale = (MLA_NOPE + MLA_ROPE) ** -0.5

    def blocks(a):
        return jnp.moveaxis(a.reshape(B, nb, Q_BLOCK, *a.shape[2:]), 1, 0)

    def attend(blk):
        qn_b, qr_b = blk
        s = (jnp.einsum('bqhd,bkhd->bhqk', qn_b, kn) + jnp.einsum('bqhr,bkr->bhqk', qr_b, kr)).astype(f32) * scale
        p = jax.nn.softmax(s, axis=-1).astype(vv.dtype)
        return jnp.einsum('bhqk,bkhd->bqhd', p, vv)

    o = lax.map(attend, (blocks(qn), blocks(qr)))
    o_mla = jnp.moveaxis(o, 0, 1).reshape(B, L, MLA_WIDTH)

    g_ml, g_hy, g_mla = jnp.split(br_g, N_BRANCH, axis=-1)
    merged = (jax.nn.sigmoid(g_ml) * (o_ml @ P['w_br_ml'][l])
              + jax.nn.sigmoid(g_hy) * (o_hy @ P['w_br_hy'][l])
              + jax.nn.sigmoid(g_mla) * (o_mla @ P['w_br_mla'][l]))
    return merged @ P['w_out'][l]


def moe(h, l, P):
    f32 = jnp.float32
    scores = jax.nn.sigmoid((h @ P['w_router']).astype(f32))
    sel = scores + P['router_bias'].astype(f32)
    sel_g = sel.reshape(*sel.shape[:-1], N_GROUPS, EXPERTS_PER_GROUP)
    grp_score = jnp.sum(lax.top_k(sel_g, 2)[0], axis=-1)
    grp_mask = jax.nn.one_hot(jnp.argmax(grp_score, axis=-1), N_GROUPS, dtype=bool)[..., None]
    masked = jnp.where(grp_mask, sel_g, -jnp.inf).reshape(sel.shape)
    _, idx = lax.top_k(masked, TOP_K)
    w = jnp.take_along_axis(scores, idx, axis=-1)
    w = w / jnp.sum(w, axis=-1, keepdims=True)
    gate = jnp.sum(jax.nn.one_hot(idx, N_EXPERTS, dtype=f32) * w[..., None], axis=-2).astype(h.dtype)
    y = jnp.zeros_like(h)
    for e in range(N_EXPERTS):
        a, b = jnp.split(h @ P['w_gate_up'][l, e], 2, axis=-1)
        y = y + gate[..., e:e + 1] * ((jax.nn.silu(a) * b) @ P['w_down'][l, e])
    return y


def encoder(x, c, P):
    for l in range(DEPTH):
        mod = jax.nn.silu(c) @ P['w_ada'][l] + P['b_ada'][l]
        sh1, sc1, g1, sh2, sc2, g2 = [m[:, None, :] for m in jnp.split(mod, 6, axis=-1)]
        hn = rmsnorm(x, P['norm1_g'][l]) * (1.0 + sc1) + sh1
        x = x + g1 * mixer(hn, l, P)
        hn = rmsnorm(x, P['norm2_g'][l]) * (1.0 + sc2) + sh2
        x = x + g2 * moe(hn, l, P)
    return rmsnorm(x, P['final_g'])


def setup_inputs(seed: int = 0) -> dict:
    key = jax.random.key(seed)
    ks = iter(jax.random.split(key, 40))
    f32 = jnp.float32

    def nrm(shape, scale):
        return jax.random.normal(next(ks), shape, f32) * scale

    def gain(shape):
        return 1.0 + nrm(shape, 0.02)

    f_base = jnp.linspace(F_BIAS_LO, F_BIAS_HI, ML_HEADS, dtype=f32)
    zero_h = jnp.zeros((ML_HEADS,), f32)
    gate_base = jnp.stack([zero_h, f_base, zero_h, f_base])
    return {
        'x_prompt': nrm((BATCH, SEQ, D_MODEL), 1.0),
        'x_sample': nrm((DEC_BATCH, DEC_SEQ, D_MODEL), 1.0),
        'c_prompt': nrm((BATCH, D_MODEL), 1.0),
        'c_sample': nrm((DEC_BATCH, D_MODEL), 1.0),
        'w_ada': nrm((DEPTH, D_MODEL, 6 * D_MODEL), 0.5 * D_MODEL ** -0.5),
        'b_ada': nrm((DEPTH, 6 * D_MODEL), 0.02),
        'norm1_g': gain((DEPTH, D_MODEL)),
        'norm2_g': gain((DEPTH, D_MODEL)),
        'w_in': nrm((DEPTH, D_MODEL, IN_COLS), D_MODEL ** -0.5),
        'ml_gate_b': gate_base[None] + nrm((DEPTH, 4, ML_HEADS), 0.1),
        'ml_norm_g': gain((DEPTH, ML_WIDTH)),
        'hy_conv_w': nrm((DEPTH, 3, (HY_ORDER + 1) * HY_WIDTH), 3 ** -0.5),
        'hy_conv_b': nrm((DEPTH, (HY_ORDER + 1) * HY_WIDTH), 0.02),
        'hy_fw1': nrm((DEPTH, HY_EMB, HY_FFN), HY_EMB ** -0.5),
        'hy_fb1': nrm((DEPTH, HY_FFN), 0.02),
        'hy_fw2': nrm((DEPTH, HY_FFN, HY_FFN), HY_FFN ** -0.5),
        'hy_fb2': nrm((DEPTH, HY_FFN), 0.02),
        'hy_fw3': nrm((DEPTH, HY_FFN, HY_ORDER * 2 * HY_WIDTH), HY_FFN ** -0.5),
        'hy_freq': 1.0 + nrm((DEPTH, 2, HY_FFN), 0.1),
        'hy_log_decay': jnp.log(jax.random.uniform(next(ks), (DEPTH, HY_ORDER * 2 * HY_WIDTH), f32, HY_DECAY_MIN, HY_DECAY_MAX)),
        'hy_skip': nrm((DEPTH, HY_ORDER, HY_WIDTH), 1.0),
        'mla_q_norm_g': gain((DEPTH, MLA_Q_RANK)),
        'mla_kv_norm_g': gain((DEPTH, MLA_KV_RANK)),
        'w_uq': nrm((DEPTH, MLA_Q_RANK, MLA_HEADS * (MLA_NOPE + MLA_ROPE)), MLA_Q_RANK ** -0.5),
        'w_ukv': nrm((DEPTH, MLA_KV_RANK, MLA_HEADS * (MLA_NOPE + MLA_V)), MLA_KV_RANK ** -0.5),
        'w_br_ml': nrm((DEPTH, ML_WIDTH, D_MODEL), ML_WIDTH ** -0.5),
        'w_br_hy': nrm((DEPTH, HY_WIDTH, D_MODEL), HY_WIDTH ** -0.5),
        'w_br_mla': nrm((DEPTH, MLA_WIDTH, D_MODEL), MLA_WIDTH ** -0.5),
        'w_out': nrm((DEPTH, D_MODEL, D_MODEL), D_MODEL ** -0.5),
        'w_router': nrm((D_MODEL, N_EXPERTS), D_MODEL ** -0.5),
        'router_bias': nrm((N_EXPERTS,), 0.01),
        'w_gate_up': nrm((DEPTH, N_EXPERTS, D_MODEL, 2 * D_EXPERT), D_MODEL ** -0.5),
        'w_down': nrm((DEPTH, N_EXPERTS, D_EXPERT, D_MODEL), D_EXPERT ** -0.5),
        'final_g': gain((D_MODEL,)),
    }


def reference(x_prompt, x_sample, c_prompt, c_sample, w_ada, b_ada, norm1_g, norm2_g, w_in, ml_gate_b, ml_norm_g,
              hy_conv_w, hy_conv_b, hy_fw1, hy_fb1, hy_fw2, hy_fb2, hy_fw3, hy_freq, hy_log_decay, hy_skip,
              mla_q_norm_g, mla_kv_norm_g, w_uq, w_ukv, w_br_ml, w_br_hy, w_br_mla, w_out, w_router, router_bias,
              w_gate_up, w_down, final_g):
    P = dict(w_ada=w_ada, b_ada=b_ada, norm1_g=norm1_g, norm2_g=norm2_g, w_in=w_in, ml_gate_b=ml_gate_b,
             ml_norm_g=ml_norm_g, hy_conv_w=hy_conv_w, hy_conv_b=hy_conv_b, hy_fw1=hy_fw1, hy_fb1=hy_fb1,
             hy_fw2=hy_fw2, hy_fb2=hy_fb2, hy_fw3=hy_fw3, hy_freq=hy_freq, hy_log_decay=hy_log_decay,
             hy_skip=hy_skip, mla_q_norm_g=mla_q_norm_g, mla_kv_norm_g=mla_kv_norm_g, w_uq=w_uq, w_ukv=w_ukv,
             w_br_ml=w_br_ml, w_br_hy=w_br_hy, w_br_mla=w_br_mla, w_out=w_out, w_router=w_router,
             router_bias=router_bias, w_gate_up=w_gate_up, w_down=w_down, final_g=final_g)
    y_prompt = encoder(x_prompt, c_prompt, P)
    y_sample = encoder(x_sample, c_sample, P)
    return (y_prompt, y_sample)
```

```python
import functools
import math

import jax
import jax.numpy as jnp
from jax import lax
from jax.experimental import pallas as pl
from jax.experimental.pallas import tpu as pltpu

F32 = jnp.float32
BF16 = jnp.bfloat16
U32 = jnp.uint32
I32 = jnp.int32

D_MODEL = 2048
ML_HEADS = 4
ML_HEAD_DIM = 256
ML_WIDTH = ML_HEADS * ML_HEAD_DIM
ML_CHUNK = 128
HY_WIDTH = 1024
HY_ORDER = 2
HY_BANDS = 16
HY_FFN = 64
MLA_HEADS = 8
MLA_Q_RANK = 512
MLA_KV_RANK = 256
MLA_NOPE = 128
MLA_ROPE = 64
MLA_V = 128
MLA_WIDTH = MLA_HEADS * MLA_V
MLA_QK = MLA_NOPE + MLA_ROPE
ROPE_BASE = 10000.0
N_EXPERTS = 16
N_GROUPS = 4
EXPERTS_PER_GROUP = N_EXPERTS // N_GROUPS
D_EXPERT = 1024
EPS = 1e-6

LANES = 128
SUBLANES = 8
VMEM_LIMIT = 56 * 1024 * 1024

COL_ML = 0
COL_HY = 4 * ML_WIDTH
COL_BRG = COL_HY + 3 * HY_WIDTH
COL_CQ = COL_BRG + 3 * D_MODEL
COL_CKV = COL_CQ + MLA_Q_RANK
COL_SIDE = COL_CKV + MLA_KV_RANK
SIDE_W = 256
SIDE_GATE = 128
IN_COLS_R = COL_SIDE + SIDE_W
IN_TN = 512

FFT_N2 = 128
FFT_TPAD = SUBLANES
FFT_KPAD = SUBLANES

MOE_TILE = 256
GATHER_TILE = 256


def _cparams(*sem):
    return pltpu.CompilerParams(dimension_semantics=sem, vmem_limit_bytes=VMEM_LIMIT)


def _rms(x, g):
    return x * lax.rsqrt(jnp.mean(x * x, axis=-1, keepdims=True) + EPS) * g


def _log_sigmoid(x):
    return -(jnp.maximum(-x, 0.0) + jnp.log1p(jnp.exp(-jnp.abs(x))))


def _sigmoid(x):
    return 1.0 / (1.0 + jnp.exp(-x))


def _dot(a, b, **kw):
    return jnp.dot(a, b, preferred_element_type=F32, **kw)


def _dot_nt(a, b, **kw):
    return lax.dot_general(a, b, (((1,), (1,)), ((), ())), preferred_element_type=F32, **kw)


HI = lax.Precision.HIGHEST


def _ada_kernel(c_ref, w_ref, b_ref, o_ref):
    c = c_ref[...]
    s = c * _sigmoid(c)
    o_ref[...] = _dot(s.astype(BF16), w_ref[...].astype(BF16)) + b_ref[...]


def _ada(c_all, w_ada, b_ada, l):
    bp = c_all.shape[0]
    n = w_ada.shape[-1]
    tn = 1024
    return pl.pallas_call(
        _ada_kernel, name="ada_mod",
        grid=(n // tn,),
        in_specs=[
            pl.BlockSpec((bp, D_MODEL), lambda j: (0, 0)),
            pl.BlockSpec((None, D_MODEL, tn), lambda j: (l, 0, j)),
            pl.BlockSpec((None, 1, tn), lambda j: (l, 0, j)),
        ],
        out_specs=pl.BlockSpec((bp, tn), lambda j: (0, j)),
        out_shape=jax.ShapeDtypeStruct((bp, n), F32),
        compiler_params=_cparams("arbitrary"),
    )(c_all, w_ada, b_ada.reshape(b_ada.shape[0], 1, n))


def _in_kernel(x_ref, g_ref, sc_ref, sh_ref, w_ref, sb_ref, o_ref, side_ref, hn_ref, *, nj):
    j = pl.program_id(1)

    @pl.when(j == 0)
    def _():
        y = _rms(x_ref[...], g_ref[...])
        hn_ref[...] = (y * (1.0 + sc_ref[0]) + sh_ref[0]).astype(BF16)

    acc = _dot(hn_ref[...], w_ref[...])
    o_ref[...] = acc.astype(BF16)

    @pl.when(j == nj - 1)
    def _():
        side_ref[...] = acc[:, IN_TN - SIDE_W:] + sb_ref[...]


def _in_proj(x2d, g, sc, sh, w_in_r, side_b, L):
    T = x2d.shape[0]
    tm = min(1024, L)
    nj = IN_COLS_R // IN_TN
    bidx = lambda i, j: ((i * tm) // L, 0, 0)
    return pl.pallas_call(
        functools.partial(_in_kernel, nj=nj), name="in_proj",
        grid=(T // tm, nj),
        in_specs=[
            pl.BlockSpec((tm, D_MODEL), lambda i, j: (i, 0)),
            pl.BlockSpec((1, D_MODEL), lambda i, j: (0, 0)),
            pl.BlockSpec((1, 1, D_MODEL), bidx),
            pl.BlockSpec((1, 1, D_MODEL), bidx),
            pl.BlockSpec((D_MODEL, IN_TN), lambda i, j: (0, j)),
            pl.BlockSpec((1, SIDE_W), lambda i, j: (0, 0)),
        ],
        out_specs=[
            pl.BlockSpec((tm, IN_TN), lambda i, j: (i, j)),
            pl.BlockSpec((tm, SIDE_W), lambda i, j: (i, 0)),
        ],
        out_shape=[
            jax.ShapeDtypeStruct((T, IN_COLS_R), BF16),
            jax.ShapeDtypeStruct((T, SIDE_W), F32),
        ],
        scratch_shapes=[pltpu.VMEM((tm, D_MODEL), BF16)],
        compiler_params=_cparams("arbitrary", "arbitrary"),
    )(x2d, g, sc, sh, w_in_r, side_b)


def _mlstm_chunk(q, k, v, li_col, lf_col, b_col, li_row, b_row, mask, C_ref, n_ref, m_ref, idx):
    m = m_ref[idx]
    C = C_ref[idx]
    n = n_ref[idx]
    Dm = jnp.where(mask, b_col - b_row + li_row, -jnp.inf)
    inter = b_col + m
    m_row = jnp.maximum(inter, jnp.max(Dm, axis=-1, keepdims=True))
    w_inter = jnp.exp(inter - m_row)
    s = _dot_nt(q, k) * jnp.exp(Dm - m_row)
    num = w_inter * _dot(q, C.astype(BF16)) + _dot(s.astype(BF16), v)
    qn = jnp.sum(q.astype(F32) * n, axis=-1, keepdims=True)
    den = w_inter * qn + jnp.sum(s, axis=-1, keepdims=True)
    h = num / jnp.maximum(jnp.abs(den), jnp.exp(-m_row))
    bL = jnp.sum(lf_col, axis=0, keepdims=True)
    g = bL - b_col + li_col
    m_new = jnp.maximum(bL + m, jnp.max(g, axis=0, keepdims=True))
    a = jnp.exp(bL + m - m_new)
    kw = k.astype(F32) * jnp.exp(g - m_new)
    C_ref[idx] = a * C + _dot(kw.T.astype(BF16), v)
    n_ref[idx] = a * n + jnp.sum(kw, axis=0, keepdims=True)
    m_ref[idx] = m_new
    return h


def _mlstm_kernel(qf_ref, kf_ref, vf_ref, sf_ref, qb_ref, kb_ref, vb_ref, sb_ref,
                  hf_ref, hb_ref, C_ref, n_ref, m_ref):
    c = pl.program_id(1)

    @pl.when(c == 0)
    def _():
        C_ref[...] = jnp.zeros_like(C_ref)
        n_ref[...] = jnp.zeros_like(n_ref)
        m_ref[...] = jnp.zeros_like(m_ref)

    row = lax.broadcasted_iota(I32, (ML_CHUNK, ML_CHUNK), 0)
    col = lax.broadcasted_iota(I32, (ML_CHUNK, ML_CHUNK), 1)
    lower = (col <= row)
    upper = (col >= row)
    tri_l = lower.astype(F32)
    tri_u = upper.astype(F32)
    kscale = ML_HEAD_DIM ** -0.5

    for d, (q_ref, k_ref, v_ref, s_ref, o_ref) in enumerate(
            ((qf_ref, kf_ref, vf_ref, sf_ref, hf_ref), (qb_ref, kb_ref, vb_ref, sb_ref, hb_ref))):
        gsub = s_ref[:, SIDE_GATE:SIDE_GATE + LANES]
        gT = gsub.T
        lsig = _log_sigmoid(gsub)
        lsigT = _log_sigmoid(gT)
        if d == 0:
            b_all = _dot(tri_l, lsig, precision=HI)
            b_allT = _dot(lsigT, tri_u, precision=HI)
            mask = lower
        else:
            b_all = _dot(tri_u, lsig, precision=HI)
            b_allT = _dot(lsigT, tri_l, precision=HI)
            mask = upper
        for h in range(ML_HEADS):
            ci = (2 * d) * ML_HEADS + h
            cf = (2 * d + 1) * ML_HEADS + h
            hs = slice(h * ML_HEAD_DIM, (h + 1) * ML_HEAD_DIM)
            q = q_ref[:, hs]
            k = k_ref[:, hs] * kscale
            v = v_ref[:, hs]
            hout = _mlstm_chunk(
                q, k, v,
                gsub[:, ci:ci + 1], lsig[:, cf:cf + 1], b_all[:, cf:cf + 1],
                gT[ci:ci + 1, :], b_allT[cf:cf + 1, :], mask,
                C_ref, n_ref, m_ref, d * ML_HEADS + h)
            o_ref[:, hs] = hout.astype(BF16)


def _mlstm(proj, side, B, L):
    T = B * L
    nc = L // ML_CHUNK
    fwd = lambda cb: (lambda b, c: (b * nc + c, cb))
    bwd = lambda cb: (lambda b, c: (b * nc + nc - 1 - c, cb))
    blk = lambda im: pl.BlockSpec((ML_CHUNK, ML_WIDTH), im)
    sblk = lambda im: pl.BlockSpec((ML_CHUNK, SIDE_W), im)
    nst = 2 * ML_HEADS
    return pl.pallas_call(
        _mlstm_kernel, name="mlstm",
        grid=(B, nc),
        in_specs=[blk(fwd(0)), blk(fwd(1)), blk(fwd(2)), sblk(fwd(0)),
                  blk(bwd(0)), blk(bwd(1)), blk(bwd(2)), sblk(bwd(0))],
        out_specs=[blk(fwd(0)), blk(bwd(0))],
        out_shape=[jax.ShapeDtypeStruct((T, ML_WIDTH), BF16)] * 2,
        scratch_shapes=[pltpu.VMEM((nst, ML_HEAD_DIM, ML_HEAD_DIM), F32),
                        pltpu.VMEM((nst, 1, ML_HEAD_DIM), F32),
                        pltpu.VMEM((nst, 1, 1), F32)],
        compiler_params=_cparams("arbitrary", "arbitrary"),
    )(proj, proj, proj, side, proj, proj, proj, side)


class _FftPlan:
    def __init__(self, L):
        self.L = L
        self.N = 2 * L
        self.N2 = FFT_N2
        self.N1 = self.N // self.N2
        self.N1h = self.N1 // 2
        self.K1 = self.N1h + 1
        self.K1p = -(-self.K1 // SUBLANES) * SUBLANES
        self.PT = self.N2 + FFT_TPAD
        self.P2 = 2 * self.N2 + FFT_KPAD


def _fft_tables(p):
    k1 = jnp.arange(p.K1p, dtype=I32)
    n1 = jnp.arange(p.N1h, dtype=I32)
    n2 = jnp.arange(p.N2, dtype=I32)
    n = p.N2 * n1[None, :] + n2[:, None]
    ph = (k1[None, :, None] * n[:, None, :]) % p.N
    ang = ph.astype(F32) * (2.0 * math.pi / p.N)
    valid = (k1 < p.K1)[None, :, None]
    c = jnp.where(valid, jnp.cos(ang), 0.0)
    s = jnp.where(valid, jnp.sin(ang), 0.0)
    t1 = jnp.concatenate([c, -s], axis=1).astype(BF16)
    wk = jnp.where((k1 == 0) | (k1 == p.N1h), 1.0, 2.0) / p.N
    wk = jnp.where(k1 < p.K1, wk, 0.0)[None, None, :]
    ct = jnp.swapaxes(c, 1, 2) * wk
    st = jnp.swapaxes(s, 1, 2) * wk
    t3 = jnp.concatenate([ct, -st], axis=2).astype(BF16)
    a2 = ((n2[:, None] * n2[None, :]) % p.N2).astype(F32) * (2.0 * math.pi / p.N2)
    c2, s2 = jnp.cos(a2), jnp.sin(a2)
    g2f = jnp.block([[c2, s2], [-s2, c2]]).astype(BF16)
    g2i = jnp.block([[c2, -s2], [s2, c2]]).astype(BF16)
    return t1, t3, g2f, g2i


def _fft_stage1(p, tst_ref, kd_ref, t1_ref):
    def body(n2, carry):
        xs = tst_ref[pl.ds(n2, p.N1h, stride=p.PT), :]
        a = _dot(t1_ref[n2], xs.astype(BF16))
        kd_ref[pl.ds(n2, p.K1p, stride=p.P2), :] = a[:p.K1p]
        kd_ref[pl.ds(p.N2 + n2, p.K1p, stride=p.P2), :] = a[p.K1p:]
        return carry
    lax.fori_loop(0, p.N2, body, 0)


def _fft_stage3(p, kd_ref, tst_ref, t3_ref):
    def body(n2, carry):
        br = kd_ref[pl.ds(n2, p.K1p, stride=p.P2), :]
        bi = kd_ref[pl.ds(p.N2 + n2, p.K1p, stride=p.P2), :]
        bc = jnp.concatenate([br, bi], axis=0).astype(BF16)
        tst_ref[pl.ds(n2, p.N1h, stride=p.PT), :] = _dot(t3_ref[n2], bc)
        return carry
    lax.fori_loop(0, p.N2, body, 0)


def _hyfeat_kernel(w1t_ref, w1c_ref, w1s_ref, b1_ref, w2_ref, b2_ref, fr_ref, o_ref, *, L, rows):
    i = pl.program_id(0)
    t = (lax.broadcasted_iota(I32, (rows, 1), 0) + i * rows).astype(F32) / L
    bands = (lax.broadcasted_iota(I32, (1, LANES), 1) + 1).astype(F32)
    ang = (2.0 * math.pi * t) * bands
    z = (t * w1t_ref[...] + _dot(jnp.cos(ang), w1c_ref[...], precision=HI)
         + _dot(jnp.sin(ang), w1s_ref[...], precision=HI) + b1_ref[...])
    h = jnp.sin(fr_ref[0:1, :] * z)
    h = jnp.sin(fr_ref[1:2, :] * (_dot(h, w2_ref[...], precision=HI) + b2_ref[...]))
    o_ref[...] = h


def _hyena_features(L, fw1, fb1, fw2, fb2, freq):
    rows = min(512, L)
    w1c = jnp.zeros((LANES, HY_FFN), F32).at[:HY_BANDS].set(fw1[1:1 + HY_BANDS])
    w1s = jnp.zeros((LANES, HY_FFN), F32).at[:HY_BANDS].set(fw1[1 + HY_BANDS:])
    full = lambda shp: pl.BlockSpec(shp, lambda i: (0,) * len(shp))
    return pl.pallas_call(
        functools.partial(_hyfeat_kernel, L=L, rows=rows), name="hy_feat",
        grid=(L // rows,),
        in_specs=[full((1, HY_FFN)), full((LANES, HY_FFN)), full((LANES, HY_FFN)), full((1, HY_FFN)),
                  full((HY_FFN, HY_FFN)), full((1, HY_FFN)), full((2, HY_FFN))],
        out_specs=pl.BlockSpec((rows, HY_FFN), lambda i: (i, 0)),
        out_shape=jax.ShapeDtypeStruct((L, HY_FFN), F32),
        compiler_params=_cparams("arbitrary"),
    )(fw1[0:1], w1c, w1s, fb1.reshape(1, -1), fw2, fb2.reshape(1, -1), freq)


def _hyfilt_kernel(h2_ref, wf_ref, wb_ref, df_ref, db_ref, t1_ref, g2_ref, o_ref,
                   tsf_ref, tsb_ref, kdf_ref, kdb_ref, *, p):
    L, N2 = p.L, p.N2
    decay_f = jnp.exp(df_ref[...])
    decay_b = jnp.exp(db_ref[...])

    def gen(n1, carry):
        r0 = pl.multiple_of(n1 * N2, N2)
        t = (lax.broadcasted_iota(I32, (N2, 1), 0) + r0).astype(F32) / L
        h2 = h2_ref[pl.ds(r0, N2), :]
        f = _dot(h2, wf_ref[...], precision=HI) * jnp.exp(-t * decay_f)
        b = _dot(h2, wb_ref[...], precision=HI) * jnp.exp(-t * decay_b)
        b = jnp.where(t == 0.0, 0.0, b)
        s0 = pl.multiple_of(n1 * p.PT, SUBLANES)
        tsf_ref[pl.ds(s0, N2), :] = f
        tsb_ref[pl.ds(s0, N2), :] = b
        return carry + jnp.sum(jnp.abs(f) + jnp.abs(b), axis=0, keepdims=True)

    l1 = lax.fori_loop(0, p.N1h, gen, jnp.zeros((1, LANES), F32))
    inv = 1.0 / l1
    _fft_stage1(p, tsf_ref, kdf_ref, t1_ref)
    _fft_stage1(p, tsb_ref, kdb_ref, t1_ref)

    def spec(k1, carry):
        s0 = pl.multiple_of(k1 * p.P2, SUBLANES)
        sf = kdf_ref[pl.ds(s0, 2 * N2), :].astype(BF16)
        sb = kdb_ref[pl.ds(s0, 2 * N2), :].astype(BF16)
        xf = _dot(g2_ref[...], sf)
        xb = _dot(g2_ref[...], sb)
        kr = (xf[:N2] + xb[:N2]) * inv
        ki = (xf[N2:] - xb[N2:]) * inv
        o_ref[k1] = jnp.concatenate([kr, ki], axis=0).astype(BF16)
        return carry
    lax.fori_loop(0, p.K1, spec, 0)


def _hyena_spectra(p, h2, fw3, log_decay, t1, g2f):
    nct = HY_WIDTH // LANES
    col = lambda d: (lambda o, c: (0, (2 * o + d) * nct + c))
    full = lambda shp: pl.BlockSpec(shp, lambda o, c: (0,) * len(shp))
    ld = log_decay.reshape(1, -1)
    return pl.pallas_call(
        functools.partial(_hyfilt_kernel, p=p), name="hy_spectra",
        grid=(HY_ORDER, nct),
        in_specs=[full((p.L, HY_FFN)),
                  pl.BlockSpec((HY_FFN, LANES), col(0)), pl.BlockSpec((HY_FFN, LANES), col(1)),
                  pl.BlockSpec((1, LANES), col(0)), pl.BlockSpec((1, LANES), col(1)),
                  full(t1.shape), full(g2f.shape)],
        out_specs=pl.BlockSpec((None, p.K1, 2 * p.N2, LANES), lambda o, c: (o, 0, 0, c)),
        out_shape=jax.ShapeDtypeStruct((HY_ORDER, p.K1, 2 * p.N2, HY_WIDTH), BF16),
        scratch_shapes=[pltpu.VMEM((p.N1h * p.PT, LANES), F32), pltpu.VMEM((p.N1h * p.PT, LANES), F32),
                        pltpu.VMEM((p.K1p * p.P2, LANES), F32), pltpu.VMEM((p.K1p * p.P2, LANES), F32)],
        compiler_params=_cparams("arbitrary", "arbitrary"),
    )(h2, fw3, fw3, ld, ld, t1, g2f)


def _conv3_chunk(u_ref, cw_ref, n1, nchunks, L):
    N2 = FFT_N2
    pk = 2 * SUBLANES
    r0 = pl.multiple_of(n1 * N2, N2)
    cur = u_ref[pl.ds(r0, N2), :].astype(F32)
    pstart = pl.multiple_of(jnp.maximum(r0 - pk, 0), pk)
    nstart = pl.multiple_of(jnp.minimum(r0 + N2, L - pk), pk)
    prev = u_ref[pl.ds(pstart, pk), :].astype(F32)[pk - 1:pk, :]
    nxt = u_ref[pl.ds(nstart, pk), :].astype(F32)[0:1, :]
    prev = prev * jnp.where(n1 > 0, 1.0, 0.0)
    nxt = nxt * jnp.where(n1 < nchunks - 1, 1.0, 0.0)
    row = lax.broadcasted_iota(I32, (N2, LANES), 0)
    up = jnp.where(row == 0, prev, pltpu.roll(cur, 1, axis=0))
    dn = jnp.where(row == N2 - 1, nxt, pltpu.roll(cur, N2 - 1, axis=0))
    return cw_ref[0:1, :] * up + cw_ref[1:2, :] * cur + cw_ref[2:3, :] * dn + cw_ref[3:4, :]


def _hyconv_kernel(ua_ref, ub_ref, cwa_ref, cwb_ref, skip_ref, sp_ref, t1_ref, t3_ref, g2f_ref, g2i_ref,
                   o_ref, tst_ref, kd_ref, *, p, pre_a):
    L, N2 = p.L, p.N2

    def chunk_a(n1):
        if pre_a:
            return _conv3_chunk(ua_ref, cwa_ref, n1, p.N1h, L)
        return ua_ref[pl.ds(pl.multiple_of(n1 * N2, N2), N2), :].astype(F32)

    def load(n1, carry):
        tst_ref[pl.ds(pl.multiple_of(n1 * p.PT, SUBLANES), N2), :] = chunk_a(n1)
        return carry
    lax.fori_loop(0, p.N1h, load, 0)

    _fft_stage1(p, tst_ref, kd_ref, t1_ref)

    def mid(k1, carry):
        s0 = pl.multiple_of(k1 * p.P2, SUBLANES)
        x = _dot(g2f_ref[...], kd_ref[pl.ds(s0, 2 * N2), :].astype(BF16))
        xr, xi = x[:N2], x[N2:]
        kk = sp_ref[k1].astype(F32)
        kr, ki = kk[:N2], kk[N2:]
        y = jnp.concatenate([xr * kr - xi * ki, xr * ki + xi * kr], axis=0).astype(BF16)
        kd_ref[pl.ds(s0, 2 * N2), :] = _dot(g2i_ref[...], y)
        return carry
    lax.fori_loop(0, p.K1, mid, 0)

    _fft_stage3(p, kd_ref, tst_ref, t3_ref)

    def store(n1, carry):
        y = tst_ref[pl.ds(pl.multiple_of(n1 * p.PT, SUBLANES), N2), :]
        a = chunk_a(n1)
        g = _conv3_chunk(ub_ref, cwb_ref, n1, p.N1h, L)
        o_ref[pl.ds(pl.multiple_of(n1 * N2, N2), N2), :] = (g * (y + skip_ref[...] * a)).astype(BF16)
        return carry
    lax.fori_loop(0, p.N1h, store, 0)


def _hyconv(p, a_arr, a_col, b_arr, b_col, cw_pack, skip, spectra, order, tables, B, pre_a):
    L = p.L
    T = B * L
    nct = HY_WIDTH // LANES
    t1, t3, g2f, g2i = tables
    full = lambda shp: pl.BlockSpec(shp, lambda c, b: (0,) * len(shp))
    acb, bcb = a_col // LANES, b_col // LANES
    cwa_cb = (a_col - COL_HY) // LANES if pre_a else 0
    cwb_cb = (b_col - COL_HY) // LANES
    return pl.pallas_call(
        functools.partial(_hyconv_kernel, p=p, pre_a=pre_a), name="hy_conv",
        grid=(nct, B),
        in_specs=[pl.BlockSpec((L, LANES), lambda c, b: (b, acb + c)),
                  pl.BlockSpec((L, LANES), lambda c, b: (b, bcb + c)),
                  pl.BlockSpec((SUBLANES, LANES), lambda c, b: (0, cwa_cb + c)),
                  pl.BlockSpec((SUBLANES, LANES), lambda c, b: (0, cwb_cb + c)),
                  pl.BlockSpec((None, 1, LANES), lambda c, b: (order, 0, c)),
                  pl.BlockSpec((None, p.K1, 2 * p.N2, LANES), lambda c, b: (order, 0, 0, c)),
                  full(t1.shape), full(t3.shape), full(g2f.shape), full(g2i.shape)],
        out_specs=pl.BlockSpec((L, LANES), lambda c, b: (b, c)),
        out_shape=jax.ShapeDtypeStruct((T, HY_WIDTH), BF16),
        scratch_shapes=[pltpu.VMEM((p.N1h * p.PT, LANES), F32), pltpu.VMEM((p.K1p * p.P2, LANES), F32)],
        compiler_params=_cparams("arbitrary", "arbitrary"),
    )(a_arr, b_arr, cw_pack, cw_pack, skip.reshape(HY_ORDER, 1, HY_WIDTH), spectra, t1, t3, g2f, g2i)


def _mlaprep_kernel(cq_ref, ckv_ref, side_ref, cs_ref, gq_ref, gkv_ref, wuq_ref, wukv_ref,
                    q_ref, k_ref, v_ref):
    cqn = _rms(cq_ref[...].astype(F32), gq_ref[...]).astype(BF16)
    ckvn = _rms(ckv_ref[...].astype(F32), gkv_ref[...]).astype(BF16)
    qa = _dot(cqn, wuq_ref[...])
    kv = _dot(ckvn, wukv_ref[...])
    cs = cs_ref[...]
    cos2, sin2 = cs[:, :MLA_ROPE], cs[:, MLA_ROPE:]
    side = side_ref[...]
    k_rope = (side[:, :MLA_ROPE] * cos2 + side[:, MLA_ROPE:2 * MLA_ROPE] * sin2).astype(BF16)
    scale = MLA_QK ** -0.5
    hw = MLA_NOPE + 2 * MLA_ROPE
    for h in range(MLA_HEADS):
        b0 = h * hw
        q_rope = qa[:, b0 + MLA_NOPE:b0 + MLA_QK] * cos2 + qa[:, b0 + MLA_QK:b0 + hw] * sin2
        q_ref[h, :, :MLA_NOPE] = (qa[:, b0:b0 + MLA_NOPE] * scale).astype(BF16)
        q_ref[h, :, MLA_NOPE:] = (q_rope * scale).astype(BF16)
        k_ref[h, :, :MLA_NOPE] = kv[:, b0:b0 + MLA_NOPE].astype(BF16)
        k_ref[h, :, MLA_NOPE:] = k_rope
        v_ref[h] = kv[:, b0 + MLA_NOPE:b0 + MLA_NOPE + MLA_V].astype(BF16)


def _mla_prep(proj, side, cs, gq, gkv, wuq_ext, wukv, B, L):
    T = B * L
    tm = min(512, L)
    nb = L // tm
    hw = MLA_NOPE + 2 * MLA_ROPE
    full = lambda shp: pl.BlockSpec(shp, lambda i: (0,) * len(shp))
    oidx = lambda i: (i // nb, 0, i % nb, 0)
    return pl.pallas_call(
        _mlaprep_kernel, name="mla_prep",
        grid=(T // tm,),
        in_specs=[pl.BlockSpec((tm, MLA_Q_RANK), lambda i: (i, COL_CQ // MLA_Q_RANK)),
                  pl.BlockSpec((tm, MLA_KV_RANK), lambda i: (i, COL_CKV // MLA_KV_RANK)),
                  pl.BlockSpec((tm, SIDE_W), lambda i: (i, 0)),
                  pl.BlockSpec((tm, 2 * MLA_ROPE), lambda i: (i % nb, 0)),
                  full((1, MLA_Q_RANK)), full((1, MLA_KV_RANK)),
                  full((MLA_Q_RANK, MLA_HEADS * hw)), full((MLA_KV_RANK, MLA_HEADS * (MLA_NOPE + MLA_V)))],
        out_specs=[pl.BlockSpec((None, MLA_HEADS, tm, MLA_QK), oidx),
                   pl.BlockSpec((None, MLA_HEADS, tm, MLA_QK), oidx),
                   pl.BlockSpec((None, MLA_HEADS, tm, MLA_V), oidx)],
        out_shape=[jax.ShapeDtypeStruct((B, MLA_HEADS, L, MLA_QK), BF16),
                   jax.ShapeDtypeStruct((B, MLA_HEADS, L, MLA_QK), BF16),
                   jax.ShapeDtypeStruct((B, MLA_HEADS, L, MLA_V), BF16)],
        compiler_params=_cparams("arbitrary"),
    )(proj, proj, side, cs, gq, gkv, wuq_ext, wukv)


def _flash_kernel(q_ref, k_ref, v_ref, o_ref, m_ref, l_ref, acc_ref, *, nkv):
    kv = pl.program_id(3)

    @pl.when(kv == 0)
    def _():
        m_ref[...] = jnp.full_like(m_ref, -jnp.inf)
        l_ref[...] = jnp.zeros_like(l_ref)
        acc_ref[...] = jnp.zeros_like(acc_ref)

    s = _dot_nt(q_ref[...], k_ref[...])
    m_prev = m_ref[...]
    m_new = jnp.maximum(m_prev, jnp.max(s, axis=-1, keepdims=True))
    alpha = jnp.exp(m_prev - m_new)
    pr = jnp.exp(s - m_new)
    l_ref[...] = alpha * l_ref[...] + jnp.sum(pr, axis=-1, keepdims=True)
    acc_ref[...] = alpha * acc_ref[...] + _dot(pr.astype(BF16), v_ref[...])
    m_ref[...] = m_new

    @pl.when(kv == nkv - 1)
    def _():
        o_ref[...] = (acc_ref[...] / l_ref[...]).astype(BF16)


def _flash(q, k, v, B, L):
    tq = min(512, L)
    tk = min(1024, L)
    nq, nkv = L // tq, L // tk
    return pl.pallas_call(
        functools.partial(_flash_kernel, nkv=nkv), name="mla_attn",
        grid=(B, MLA_HEADS, nq, nkv),
        in_specs=[pl.BlockSpec((None, None, tq, MLA_QK), lambda b, h, i, j: (b, h, i, 0)),
                  pl.BlockSpec((None, None, tk, MLA_QK), lambda b, h, i, j: (b, h, j, 0)),
                  pl.BlockSpec((None, None, tk, MLA_V), lambda b, h, i, j: (b, h, j, 0))],
        out_specs=pl.BlockSpec((tq, MLA_V), lambda b, h, i, j: (b * nq + i, h)),
        out_shape=jax.ShapeDtypeStruct((B * L, MLA_WIDTH), BF16),
        scratch_shapes=[pltpu.VMEM((tq, 1), F32), pltpu.VMEM((tq, 1), F32), pltpu.VMEM((tq, MLA_V), F32)],
        compiler_params=_cparams("arbitrary", "arbitrary", "arbitrary", "arbitrary"),
    )(q, k, v)


def _merge_kernel(hf_ref, hb_ref, mlo_ref, mlg_ref, ohy_ref, omla_ref, gml_ref, ghy_ref, gmla_ref,
                  wml_ref, why_ref, wmla_ref, o_ref, oml_ref):
    @pl.when(pl.program_id(1) == 0)
    def _():
        for h in range(ML_HEADS):
            hs = slice(h * ML_HEAD_DIM, (h + 1) * ML_HEAD_DIM)
            x = hf_ref[:, hs].astype(F32) + hb_ref[:, hs].astype(F32)
            y = _rms(x, mlg_ref[:, hs])
            oml_ref[:, hs] = (y * _sigmoid(mlo_ref[:, hs].astype(F32))).astype(BF16)

    acc = _sigmoid(gml_ref[...].astype(F32)) * _dot(oml_ref[...], wml_ref[...])
    acc += _sigmoid(ghy_ref[...].astype(F32)) * _dot(ohy_ref[...], why_ref[...])
    acc += _sigmoid(gmla_ref[...].astype(F32)) * _dot(omla_ref[...], wmla_ref[...])
    o_ref[...] = acc.astype(BF16)


def _merge(hf, hb, proj, ml_norm_g, o_hy, o_mla, wml, why, wmla, L):
    T = hf.shape[0]
    tm = min(1024, L)
    tn = 512
    row = lambda cb: (lambda i, j: (i, cb))
    gate = lambda k: (lambda i, j: (i, (COL_BRG + k * D_MODEL) // tn + j))
    wsp = pl.BlockSpec((ML_WIDTH, tn), lambda i, j: (0, j))
    return pl.pallas_call(
        _merge_kernel, name="merge",
        grid=(T // tm, D_MODEL // tn),
        in_specs=[pl.BlockSpec((tm, ML_WIDTH), row(0)), pl.BlockSpec((tm, ML_WIDTH), row(0)),
                  pl.BlockSpec((tm, ML_WIDTH), row(3)),
                  pl.BlockSpec((1, ML_WIDTH), lambda i, j: (0, 0)),
                  pl.BlockSpec((tm, HY_WIDTH), row(0)), pl.BlockSpec((tm, MLA_WIDTH), row(0)),
                  pl.BlockSpec((tm, tn), gate(0)), pl.BlockSpec((tm, tn), gate(1)), pl.BlockSpec((tm, tn), gate(2)),
                  wsp, wsp, wsp],
        out_specs=pl.BlockSpec((tm, tn), lambda i, j: (i, j)),
        out_shape=jax.ShapeDtypeStruct((T, D_MODEL), BF16),
        scratch_shapes=[pltpu.VMEM((tm, ML_WIDTH), BF16)],
        compiler_params=_cparams("arbitrary", "arbitrary"),
    )(hf, hb, proj, ml_norm_g, o_hy, o_mla, proj, proj, proj, wml, why, wmla)


def _outproj_kernel(m_ref, w_ref, x_ref, g_ref, o_ref):
    o_ref[...] = x_ref[...] + g_ref[0] * _dot(m_ref[...], w_ref[...])


def _out_proj(merged, w_out, x2d, g1, L):
    T = x2d.shape[0]
    tm = min(1024, L)
    tn = 512
    return pl.pallas_call(
        _outproj_kernel, name="out_proj",
        grid=(T // tm, D_MODEL // tn),
        in_specs=[pl.BlockSpec((tm, D_MODEL), lambda i, j: (i, 0)),
                  pl.BlockSpec((D_MODEL, tn), lambda i, j: (0, j)),
                  pl.BlockSpec((tm, tn), lambda i, j: (i, j)),
                  pl.BlockSpec((1, 1, tn), lambda i, j: ((i * tm) // L, 0, j))],
        out_specs=pl.BlockSpec((tm, tn), lambda i, j: (i, j)),
        out_shape=jax.ShapeDtypeStruct((T, D_MODEL), F32),
        compiler_params=_cparams("arbitrary", "arbitrary"),
    )(merged, w_out, x2d, g1)


def _pack_bf16_pair(lo, hi):
    lb = lax.bitcast_convert_type(lo.astype(BF16).astype(F32), U32)
    hb = lax.bitcast_convert_type(hi.astype(BF16).astype(F32), U32)
    return hb | (lb >> 16)


def _unpack_bf16_pair(w):
    lo = lax.bitcast_convert_type(w << 16, F32)
    hi = lax.bitcast_convert_type(w & jnp.uint32(0xFFFF0000), F32)
    return lo, hi


def _route_kernel(x_ref, g_ref, sc_ref, sh_ref, wr_ref, rb_ref, hp_ref, idx_ref, wgt_ref):
    half = D_MODEL // 2
    hn = _rms(x_ref[...], g_ref[...]) * (1.0 + sc_ref[0]) + sh_ref[0]
    hp_ref[...] = _pack_bf16_pair(hn[:, :half], hn[:, half:])
    logits = _dot_nt(wr_ref[...], hn, precision=HI)
    scores = _sigmoid(logits)
    sel = scores + rb_ref[...]
    rows = [sel[e:e + 1, :] for e in range(N_EXPERTS)]
    srow = [scores[e:e + 1, :] for e in range(N_EXPERTS)]
    gs = []
    for g in range(N_GROUPS):
        r = rows[g * EXPERTS_PER_GROUP:(g + 1) * EXPERTS_PER_GROUP]
        best = None
        for a in range(EXPERTS_PER_GROUP):
            for b in range(a + 1, EXPERTS_PER_GROUP):
                s = r[a] + r[b]
                best = s if best is None else jnp.maximum(best, s)
        gs.append(best)
    gsel = jnp.zeros_like(gs[0], dtype=I32)
    gbest = gs[0]
    for g in range(1, N_GROUPS):
        better = gs[g] > gbest
        gsel = jnp.where(better, g, gsel)
        gbest = jnp.where(better, gs[g], gbest)
    masked = [jnp.where(gsel == (e // EXPERTS_PER_GROUP), rows[e], -jnp.inf) for e in range(N_EXPERTS)]
    picks = []
    for _ in range(2):
        bi = jnp.zeros_like(gsel)
        bv = masked[0]
        bs = srow[0]
        for e in range(1, N_EXPERTS):
            better = masked[e] > bv
            bi = jnp.where(better, e, bi)
            bv = jnp.where(better, masked[e], bv)
            bs = jnp.where(better, srow[e], bs)
        picks.append((bi, bs))
        masked = [jnp.where(bi == e, -jnp.inf, masked[e]) for e in range(N_EXPERTS)]
    (i1, s1), (i2, s2) = picks
    tot = s1 + s2
    idx_ref[...] = jnp.concatenate([i1, i2], axis=0)
    wgt_ref[...] = jnp.concatenate([s1 / tot, s2 / tot], axis=0)


def _route(x2d, g, sc, sh, w_router, router_bias, L):
    T = x2d.shape[0]
    tm = min(512, L)
    bidx = lambda i: ((i * tm) // L, 0, 0)
    return pl.pallas_call(
        _route_kernel, name="route",
        grid=(T // tm,),
        in_specs=[pl.BlockSpec((tm, D_MODEL), lambda i: (i, 0)),
                  pl.BlockSpec((1, D_MODEL), lambda i: (0, 0)),
                  pl.BlockSpec((1, 1, D_MODEL), bidx), pl.BlockSpec((1, 1, D_MODEL), bidx),
                  pl.BlockSpec((N_EXPERTS, D_MODEL), lambda i: (0, 0)),
                  pl.BlockSpec((N_EXPERTS, 1), lambda i: (0, 0))],
        out_specs=[pl.BlockSpec((tm, D_MODEL // 2), lambda i: (i, 0)),
                   pl.BlockSpec((2, tm), lambda i: (0, i)),
                   pl.BlockSpec((2, tm), lambda i: (0, i))],
        out_shape=[jax.ShapeDtypeStruct((T, D_MODEL // 2), U32),
                   jax.ShapeDtypeStruct((2, T), I32),
                   jax.ShapeDtypeStruct((2, T), F32)],
        compiler_params=_cparams("arbitrary"),
    )(x2d, g, sc, sh, w_router.T, router_bias.reshape(N_EXPERTS, 1))


def _moe_plan(idx, T):
    A = 2 * T
    e = idx.reshape(A)
    tok = jnp.tile(jnp.arange(T, dtype=I32), 2)
    onehot = (e[:, None] == jnp.arange(N_EXPERTS, dtype=I32)[None, :]).astype(I32)
    csum = jnp.cumsum(onehot, axis=0)
    counts = csum[-1]
    rank = jnp.sum(csum * onehot, axis=1) - 1
    padded = ((counts + MOE_TILE - 1) // MOE_TILE) * MOE_TILE
    ends = jnp.cumsum(padded)
    starts = ends - padded
    pos = jnp.sum(starts[None, :] * onehot, axis=1) + rank
    R = A + N_EXPERTS * MOE_TILE
    src = jnp.zeros((R,), I32).at[pos].set(tok)
    n_tiles = R // MOE_TILE
    tile_start = jnp.arange(n_tiles, dtype=I32) * MOE_TILE
    tile_e = jnp.minimum(jnp.sum((tile_start[:, None] >= ends[None, :]).astype(I32), axis=1), N_EXPERTS - 1)
    num_used = (ends[-1] // MOE_TILE).astype(I32).reshape(1)
    return src, pos.reshape(2, T).astype(I32), tile_e.astype(I32), num_used


def _row_gather_copy(src_hbm, dst_ref, sem, src_row, dst_row):
    return pltpu.make_async_copy(src_hbm.at[pl.ds(src_row, 1)], dst_ref.at[pl.ds(dst_row, 1)], sem)


def _gather_kernel(nt_ref, idx_ref, src_hbm, o_ref, sem, *, rows):
    @pl.when(pl.program_id(0) < nt_ref[0])
    def _():
        def issue(r, carry):
            _row_gather_copy(src_hbm, o_ref, sem, idx_ref[0, 0, r], r).start()
            return carry
        lax.fori_loop(0, rows, issue, 0)

        def drain(r, carry):
            _row_gather_copy(src_hbm, o_ref, sem, 0, r).wait()
            return carry
        lax.fori_loop(0, rows, drain, 0)

    @pl.when(pl.program_id(0) >= nt_ref[0])
    def _():
        o_ref[...] = jnp.zeros_like(o_ref)


def _gather_rows(table, src, num_used):
    R = src.shape[0]
    W = table.shape[1]
    nt = R // GATHER_TILE
    grid_spec = pltpu.PrefetchScalarGridSpec(
        num_scalar_prefetch=1,
        grid=(nt,),
        in_specs=[pl.BlockSpec((1, 1, GATHER_TILE), lambda i, n: (i, 0, 0), memory_space=pltpu.SMEM),
                  pl.BlockSpec(memory_space=pl.ANY)],
        out_specs=pl.BlockSpec((GATHER_TILE, W), lambda i, n: (i, 0)),
        scratch_shapes=[pltpu.SemaphoreType.DMA(())],
    )
    return pl.pallas_call(
        functools.partial(_gather_kernel, rows=GATHER_TILE), name="moe_gather",
        grid_spec=grid_spec,
        out_shape=jax.ShapeDtypeStruct((R, W), table.dtype),
        compiler_params=_cparams("arbitrary"),
    )(num_used, src.reshape(nt, 1, GATHER_TILE), table)


def _expert_kernel(te_ref, nt_ref, x_ref, wgu_ref, wd_ref, o_ref):
    @pl.when(pl.program_id(0) < nt_ref[0])
    def _():
        half = D_MODEL // 2
        lo, hi = _unpack_bf16_pair(x_ref[...])
        h1 = _dot(lo.astype(BF16), wgu_ref[:half, :]) + _dot(hi.astype(BF16), wgu_ref[half:, :])
        a, b = h1[:, :D_EXPERT], h1[:, D_EXPERT:]
        act = (a * _sigmoid(a) * b).astype(BF16)
        y = _dot(act, wd_ref[...])
        o_ref[...] = _pack_bf16_pair(y[:, :half], y[:, half:])

    @pl.when(pl.program_id(0) >= nt_ref[0])
    def _():
        o_ref[...] = jnp.zeros_like(o_ref)


def _experts(xs, tile_e, num_used, wgu, wd, l):
    R = xs.shape[0]
    nt = R // MOE_TILE
    half = D_MODEL // 2
    clamp = lambda t, te, n: (t, 0)
    grid_spec = pltpu.PrefetchScalarGridSpec(
        num_scalar_prefetch=2,
        grid=(nt,),
        in_specs=[pl.BlockSpec((MOE_TILE, half), clamp),
                  pl.BlockSpec((None, None, D_MODEL, 2 * D_EXPERT), lambda t, te, n: (l, te[t], 0, 0)),
                  pl.BlockSpec((None, None, D_EXPERT, D_MODEL), lambda t, te, n: (l, te[t], 0, 0))],
        out_specs=pl.BlockSpec((MOE_TILE, half), clamp),
    )
    return pl.pallas_call(
        _expert_kernel, name="moe_experts",
        grid_spec=grid_spec,
        out_shape=jax.ShapeDtypeStruct((R, half), U32),
        compiler_params=_cparams("arbitrary"),
    )(tile_e, num_used, xs, wgu, wd)


def _combine_kernel(p1_ref, p2_ref, ys_hbm, w_ref, x_ref, g_ref, fg_ref, o_ref, buf_ref, sem, *, rows, final):
    def issue(r, carry):
        _row_gather_copy(ys_hbm, buf_ref.at[0], sem, p1_ref[0, 0, r], r).start()
        _row_gather_copy(ys_hbm, buf_ref.at[1], sem, p2_ref[0, 0, r], r).start()
        return carry
    lax.fori_loop(0, rows, issue, 0)

    def drain(r, carry):
        _row_gather_copy(ys_hbm, buf_ref.at[0], sem, 0, r).wait()
        _row_gather_copy(ys_hbm, buf_ref.at[1], sem, 0, r).wait()
        return carry
    lax.fori_loop(0, rows, drain, 0)

    w = w_ref[...]
    w1, w2 = w[:, 0:1], w[:, 1:2]
    lo1, hi1 = _unpack_bf16_pair(buf_ref[0])
    lo2, hi2 = _unpack_bf16_pair(buf_ref[1])
    half = D_MODEL // 2
    g = g_ref[0]
    xlo = x_ref[:, :half] + g[:, :half] * (w1 * lo1 + w2 * lo2)
    xhi = x_ref[:, half:] + g[:, half:] * (w1 * hi1 + w2 * hi2)
    if final:
        ms = (jnp.sum(xlo * xlo, axis=-1, keepdims=True) + jnp.sum(xhi * xhi, axis=-1, keepdims=True)) / D_MODEL
        r = lax.rsqrt(ms + EPS)
        xlo = xlo * r * fg_ref[:, :half]
        xhi = xhi * r * fg_ref[:, half:]
    o_ref[:, :half] = xlo
    o_ref[:, half:] = xhi


def _combine(ys, pos, wgt, x2d, g2, final_g, L, final):
    T = x2d.shape[0]
    rows = GATHER_TILE
    nt = T // rows
    half = D_MODEL // 2
    smem = lambda: pl.BlockSpec((1, 1, rows), lambda i: (i, 0, 0), memory_space=pltpu.SMEM)
    return pl.pallas_call(
        functools.partial(_combine_kernel, rows=rows, final=final), name="moe_combine",
        grid=(nt,),
        in_specs=[smem(), smem(),
                  pl.BlockSpec(memory_space=pl.ANY),
                  pl.BlockSpec((rows, 2), lambda i: (i, 0)),
                  pl.BlockSpec((rows, D_MODEL), lambda i: (i, 0)),
                  pl.BlockSpec((1, 1, D_MODEL), lambda i: ((i * rows) // L, 0, 0)),
                  pl.BlockSpec((1, D_MODEL), lambda i: (0, 0))],
        out_specs=pl.BlockSpec((rows, D_MODEL), lambda i: (i, 0)),
        out_shape=jax.ShapeDtypeStruct((T, D_MODEL), F32),
        scratch_shapes=[pltpu.VMEM((2, rows, half), U32), pltpu.SemaphoreType.DMA(())],
        compiler_params=_cparams("arbitrary"),
    )(pos[0].reshape(nt, 1, rows), pos[1].reshape(nt, 1, rows), ys, wgt.T, x2d, g2, final_g)


def _swap_halves(w):
    h = w.shape[-1] // 2
    return jnp.concatenate([w[..., h:], w[..., :h]], axis=-1)


def _prep_layer(P, l):
    w = P["w_in"][l]
    o_g = 4 * ML_WIDTH
    o_hy = o_g + 4 * ML_HEADS
    o_cq = o_hy + 3 * HY_WIDTH
    o_ckv = o_cq + MLA_Q_RANK
    o_kr = o_ckv + MLA_KV_RANK
    o_br = o_kr + MLA_ROPE
    kr = w[:, o_kr:o_br]
    side_pad = jnp.zeros((D_MODEL, SIDE_W - SIDE_GATE - 4 * ML_HEADS), F32)
    w_in_r = jnp.concatenate(
        [w[:, :o_g], w[:, o_hy:o_cq], w[:, o_br:], w[:, o_cq:o_ckv], w[:, o_ckv:o_kr],
         kr, _swap_halves(kr), w[:, o_g:o_hy], side_pad], axis=1).astype(BF16)
    side_b = jnp.zeros((1, SIDE_W), F32).at[0, SIDE_GATE:SIDE_GATE + 4 * ML_HEADS].set(P["ml_gate_b"][l].reshape(-1))
    wuq = P["w_uq"][l].reshape(MLA_Q_RANK, MLA_HEADS, MLA_QK)
    qr = wuq[..., MLA_NOPE:]
    wuq_ext = jnp.concatenate([wuq, _swap_halves(qr)], axis=-1).reshape(MLA_Q_RANK, -1).astype(BF16)
    cw = jnp.concatenate([P["hy_conv_w"][l], P["hy_conv_b"][l][None, :],
                          jnp.zeros((SUBLANES - 4, 3 * HY_WIDTH), F32)], axis=0)
    return dict(
        w_in_r=w_in_r, side_b=side_b, wuq_ext=wuq_ext, wukv=P["w_ukv"][l].astype(BF16), cw=cw,
        wml=P["w_br_ml"][l].astype(BF16), why=P["w_br_hy"][l].astype(BF16), wmla=P["w_br_mla"][l].astype(BF16),
        w_out=P["w_out"][l].astype(BF16),
        norm1_g=P["norm1_g"][l].reshape(1, -1), norm2_g=P["norm2_g"][l].reshape(1, -1),
        ml_norm_g=P["ml_norm_g"][l].reshape(1, -1),
        gq=P["mla_q_norm_g"][l].reshape(1, -1), gkv=P["mla_kv_norm_g"][l].reshape(1, -1),
    )


def _rope_table(L):
    pos = jnp.arange(L, dtype=F32)
    inv = ROPE_BASE ** (-jnp.arange(0, MLA_ROPE, 2, dtype=F32) / MLA_ROPE)
    ang = pos[:, None] * inv[None, :]
    cos, sin = jnp.cos(ang), jnp.sin(ang)
    return jnp.concatenate([cos, cos, -sin, sin], axis=-1)


def _encoder(x, mods, P, W, wgu_bf, wd_bf):
    B, L, _ = x.shape
    T = B * L
    x2d = x.reshape(T, D_MODEL)
    plan = _FftPlan(L)
    tables = _fft_tables(plan)
    cs = _rope_table(L)
    depth = len(W)
    for l in range(depth):
        wl = W[l]
        sh1, sc1, g1, sh2, sc2, g2 = [mods[l][:, k].reshape(B, 1, D_MODEL) for k in range(6)]
        proj, side = _in_proj(x2d, wl["norm1_g"], sc1, sh1, wl["w_in_r"], wl["side_b"], L)
        hf, hb = _mlstm(proj, side, B, L)
        h2 = _hyena_features(L, P["hy_fw1"][l], P["hy_fb1"][l], P["hy_fw2"][l], P["hy_fb2"][l], P["hy_freq"][l])
        spectra = _hyena_spectra(plan, h2, P["hy_fw3"][l], P["hy_log_decay"][l], tables[0], tables[2])
        skip = P["hy_skip"][l]
        z = _hyconv(plan, proj, COL_HY, proj, COL_HY + HY_WIDTH, wl["cw"], skip, spectra, 0, tables, B, True)
        o_hy = _hyconv(plan, z, 0, proj, COL_HY + 2 * HY_WIDTH, wl["cw"], skip, spectra, 1, tables, B, False)
        q, k, v = _mla_prep(proj, side, cs, wl["gq"], wl["gkv"], wl["wuq_ext"], wl["wukv"], B, L)
        o_mla = _flash(q, k, v, B, L)
        merged = _merge(hf, hb, proj, wl["ml_norm_g"], o_hy, o_mla, wl["wml"], wl["why"], wl["wmla"], L)
        x2d = _out_proj(merged, wl["w_out"], x2d, g1, L)
        hp, idx, wgt = _route(x2d, wl["norm2_g"], sc2, sh2, P["w_router"], P["router_bias"], L)
        src, pos, tile_e, num_used = _moe_plan(idx, T)
        xs = _gather_rows(hp, src, num_used)
        ys = _experts(xs, tile_e, num_used, wgu_bf, wd_bf, l)
        x2d = _combine(ys, pos, wgt, x2d, g2, P["final_g"].reshape(1, -1), L, final=(l == depth - 1))
    return x2d.reshape(B, L, D_MODEL)


def kernel(x_prompt, x_sample, c_prompt, c_sample, w_ada, b_ada, norm1_g, norm2_g, w_in, ml_gate_b, ml_norm_g,
           hy_conv_w, hy_conv_b, hy_fw1, hy_fb1, hy_fw2, hy_fb2, hy_fw3, hy_freq, hy_log_decay, hy_skip,
           mla_q_norm_g, mla_kv_norm_g, w_uq, w_ukv, w_br_ml, w_br_hy, w_br_mla, w_out, w_router, router_bias,
           w_gate_up, w_down, final_g):
    P = dict(w_ada=w_ada, b_ada=b_ada, norm1_g=norm1_g, norm2_g=norm2_g, w_in=w_in, ml_gate_b=ml_gate_b,
             ml_norm_g=ml_norm_g, hy_conv_w=hy_conv_w, hy_conv_b=hy_conv_b, hy_fw1=hy_fw1, hy_fb1=hy_fb1,
             hy_fw2=hy_fw2, hy_fb2=hy_fb2, hy_fw3=hy_fw3, hy_freq=hy_freq, hy_log_decay=hy_log_decay,
             hy_skip=hy_skip, mla_q_norm_g=mla_q_norm_g, mla_kv_norm_g=mla_kv_norm_g, w_uq=w_uq, w_ukv=w_ukv,
             w_br_ml=w_br_ml, w_br_hy=w_br_hy, w_br_mla=w_br_mla, w_out=w_out, w_router=w_router,
             router_bias=router_bias, w_gate_up=w_gate_up, w_down=w_down, final_g=final_g)
    depth = w_in.shape[0]
    W = [_prep_layer(P, l) for l in range(depth)]
    wgu_bf = w_gate_up.astype(BF16)
    wd_bf = w_down.astype(BF16)
    bp, bs = c_prompt.shape[0], c_sample.shape[0]
    rows = -(-(bp + bs) // SUBLANES) * SUBLANES
    c_all = jnp.zeros((rows, D_MODEL), F32).at[:bp].set(c_prompt).at[bp:bp + bs].set(c_sample)
    mods_p, mods_s = [], []
    for l in range(depth):
        mod = _ada(c_all, w_ada, b_ada, l).reshape(rows, 6, D_MODEL)
        mods_p.append(mod[:bp])
        mods_s.append(mod[bp:bp + bs])
    y_prompt = _encoder(x_prompt, mods_p, P, W, wgu_bf, wd_bf)
    y_sample = _encoder(x_sample, mods_s, P, W, wgu_bf, wd_bf)
    return (y_prompt, y_sample)
```

```python
import functools
import math

import jax
import jax.numpy as jnp
from jax import lax
from jax.experimental import pallas as pl
from jax.experimental.pallas import tpu as pltpu

F32 = jnp.float32
BF16 = jnp.bfloat16
U32 = jnp.uint32
I32 = jnp.int32

D_MODEL = 2048
ML_HEADS = 4
ML_HEAD_DIM = 256
ML_WIDTH = ML_HEADS * ML_HEAD_DIM
ML_CHUNK = 128
HY_WIDTH = 1024
HY_ORDER = 2
HY_BANDS = 16
HY_FFN = 64
MLA_HEADS = 8
MLA_Q_RANK = 512
MLA_KV_RANK = 256
MLA_NOPE = 128
MLA_ROPE = 64
MLA_V = 128
MLA_WIDTH = MLA_HEADS * MLA_V
MLA_QK = MLA_NOPE + MLA_ROPE
ROPE_BASE = 10000.0
N_EXPERTS = 16
N_GROUPS = 4
EXPERTS_PER_GROUP = N_EXPERTS // N_GROUPS
D_EXPERT = 1024
EPS = 1e-6

LANES = 128
SUBLANES = 8
VMEM_LIMIT = 56 * 1024 * 1024

COL_ML = 0
COL_HY = 4 * ML_WIDTH
COL_BRG = COL_HY + 3 * HY_WIDTH
COL_CQ = COL_BRG + 3 * D_MODEL
COL_CKV = COL_CQ + MLA_Q_RANK
COL_SIDE = COL_CKV + MLA_KV_RANK
SIDE_W = 256
SIDE_GATE = 128
IN_COLS_R = COL_SIDE + SIDE_W
IN_TN = 512

FFT_N2 = 128
FFT_TPAD = SUBLANES
FFT_KPAD = SUBLANES
FFT_UNROLL_OUTER = 16
FFT_UNROLL_MID = 8

MOE_TILE = 256
GATHER_TILE = 256
ATTN_TQ = 1024
ATTN_TK = 2048
ATTN_SPLIT = 1


def _cparams(*sem):
    return pltpu.CompilerParams(dimension_semantics=sem, vmem_limit_bytes=VMEM_LIMIT)


def _rms(x, g):
    return x * lax.rsqrt(jnp.mean(x * x, axis=-1, keepdims=True) + EPS) * g


def _log_sigmoid(x):
    return -(jnp.maximum(-x, 0.0) + jnp.log1p(jnp.exp(-jnp.abs(x))))


def _sigmoid(x):
    return 1.0 / (1.0 + jnp.exp(-x))


def _dot(a, b, **kw):
    return jnp.dot(a, b, preferred_element_type=F32, **kw)


def _dot_nt(a, b, **kw):
    return lax.dot_general(a, b, (((1,), (1,)), ((), ())), preferred_element_type=F32, **kw)


HI = lax.Precision.HIGHEST


def _ada_kernel(c_ref, w_ref, b_ref, o_ref):
    c = c_ref[...]
    s = c * _sigmoid(c)
    o_ref[...] = _dot(s.astype(BF16), w_ref[...].astype(BF16)) + b_ref[...]


def _ada(c_all, w_ada, b_ada, l):
    bp = c_all.shape[0]
    n = w_ada.shape[-1]
    tn = 1024
    return pl.pallas_call(
        _ada_kernel, name="ada_mod",
        grid=(n // tn,),
        in_specs=[
            pl.BlockSpec((bp, D_MODEL), lambda j: (0, 0)),
            pl.BlockSpec((None, D_MODEL, tn), lambda j: (l, 0, j)),
            pl.BlockSpec((None, 1, tn), lambda j: (l, 0, j)),
        ],
        out_specs=pl.BlockSpec((bp, tn), lambda j: (0, j)),
        out_shape=jax.ShapeDtypeStruct((bp, n), F32),
        compiler_params=_cparams("arbitrary"),
    )(c_all, w_ada, b_ada.reshape(b_ada.shape[0], 1, n))


def _in_kernel(x_ref, g_ref, sc_ref, sh_ref, w_ref, sb_ref, o_ref, side_ref, hn_ref, *, nj):
    j = pl.program_id(1)

    @pl.when(j == 0)
    def _():
        y = _rms(x_ref[...], g_ref[...])
        hn_ref[...] = (y * (1.0 + sc_ref[0]) + sh_ref[0]).astype(BF16)

    acc = _dot(hn_ref[...], w_ref[...])
    o_ref[...] = acc.astype(BF16)

    @pl.when(j == nj - 1)
    def _():
        side_ref[...] = acc[:, IN_TN - SIDE_W:] + sb_ref[...]


def _in_proj(x2d, g, sc, sh, w_in_r, side_b, L):
    T = x2d.shape[0]
    tm = min(1024, L)
    nj = IN_COLS_R // IN_TN
    bidx = lambda i, j: ((i * tm) // L, 0, 0)
    return pl.pallas_call(
        functools.partial(_in_kernel, nj=nj), name="in_proj",
        grid=(T // tm, nj),
        in_specs=[
            pl.BlockSpec((tm, D_MODEL), lambda i, j: (i, 0)),
            pl.BlockSpec((1, D_MODEL), lambda i, j: (0, 0)),
            pl.BlockSpec((1, 1, D_MODEL), bidx),
            pl.BlockSpec((1, 1, D_MODEL), bidx),
            pl.BlockSpec((D_MODEL, IN_TN), lambda i, j: (0, j)),
            pl.BlockSpec((1, SIDE_W), lambda i, j: (0, 0)),
        ],
        out_specs=[
            pl.BlockSpec((tm, IN_TN), lambda i, j: (i, j)),
            pl.BlockSpec((tm, SIDE_W), lambda i, j: (i, 0)),
        ],
        out_shape=[
            jax.ShapeDtypeStruct((T, IN_COLS_R), BF16),
            jax.ShapeDtypeStruct((T, SIDE_W), F32),
        ],
        scratch_shapes=[pltpu.VMEM((tm, D_MODEL), BF16)],
        compiler_params=_cparams("arbitrary", "arbitrary"),
    )(x2d, g, sc, sh, w_in_r, side_b)


def _mlstm_chunk(q, k, v, li_col, lf_col, b_col, li_row, b_row, mask, C_ref, n_ref, m_ref, idx):
    m = m_ref[idx]
    C = C_ref[idx]
    n = n_ref[idx]
    Dm = jnp.where(mask, b_col - b_row + li_row, -jnp.inf)
    inter = b_col + m
    m_row = jnp.maximum(inter, jnp.max(Dm, axis=-1, keepdims=True))
    w_inter = jnp.exp(inter - m_row)
    s = _dot_nt(q, k) * jnp.exp(Dm - m_row)
    num = w_inter * _dot(q, C.astype(BF16)) + _dot(s.astype(BF16), v)
    qn = jnp.sum(q.astype(F32) * n, axis=-1, keepdims=True)
    den = w_inter * qn + jnp.sum(s, axis=-1, keepdims=True)
    h = num / jnp.maximum(jnp.abs(den), jnp.exp(-m_row))
    bL = jnp.sum(lf_col, axis=0, keepdims=True)
    g = bL - b_col + li_col
    m_new = jnp.maximum(bL + m, jnp.max(g, axis=0, keepdims=True))
    a = jnp.exp(bL + m - m_new)
    kw = k.astype(F32) * jnp.exp(g - m_new)
    C_ref[idx] = a * C + _dot(kw.T.astype(BF16), v)
    n_ref[idx] = a * n + jnp.sum(kw, axis=0, keepdims=True)
    m_ref[idx] = m_new
    return h


def _mlstm_kernel(qf_ref, kf_ref, vf_ref, sf_ref, qb_ref, kb_ref, vb_ref, sb_ref,
                  hf_ref, hb_ref, C_ref, n_ref, m_ref):
    c = pl.program_id(1)

    @pl.when(c == 0)
    def _():
        C_ref[...] = jnp.zeros_like(C_ref)
        n_ref[...] = jnp.zeros_like(n_ref)
        m_ref[...] = jnp.zeros_like(m_ref)

    row = lax.broadcasted_iota(I32, (ML_CHUNK, ML_CHUNK), 0)
    col = lax.broadcasted_iota(I32, (ML_CHUNK, ML_CHUNK), 1)
    lower = (col <= row)
    upper = (col >= row)
    tri_l = lower.astype(F32)
    tri_u = upper.astype(F32)
    kscale = ML_HEAD_DIM ** -0.5

    for d, (q_ref, k_ref, v_ref, s_ref, o_ref) in enumerate(
            ((qf_ref, kf_ref, vf_ref, sf_ref, hf_ref), (qb_ref, kb_ref, vb_ref, sb_ref, hb_ref))):
        gsub = s_ref[:, SIDE_GATE:SIDE_GATE + LANES]
        gT = gsub.T
        lsig = _log_sigmoid(gsub)
        lsigT = _log_sigmoid(gT)
        if d == 0:
            b_all = _dot(tri_l, lsig, precision=HI)
            b_allT = _dot(lsigT, tri_u, precision=HI)
            mask = lower
        else:
            b_all = _dot(tri_u, lsig, precision=HI)
            b_allT = _dot(lsigT, tri_l, precision=HI)
            mask = upper
        for h in range(ML_HEADS):
            ci = (2 * d) * ML_HEADS + h
            cf = (2 * d + 1) * ML_HEADS + h
            hs = slice(h * ML_HEAD_DIM, (h + 1) * ML_HEAD_DIM)
            q = q_ref[:, hs]
            k = k_ref[:, hs] * kscale
            v = v_ref[:, hs]
            hout = _mlstm_chunk(
                q, k, v,
                gsub[:, ci:ci + 1], lsig[:, cf:cf + 1], b_all[:, cf:cf + 1],
                gT[ci:ci + 1, :], b_allT[cf:cf + 1, :], mask,
                C_ref, n_ref, m_ref, d * ML_HEADS + h)
            o_ref[:, hs] = hout.astype(BF16)


def _mlstm(proj, side, B, L):
    T = B * L
    nc = L // ML_CHUNK
    fwd = lambda cb: (lambda b, c: (b * nc + c, cb))
    bwd = lambda cb: (lambda b, c: (b * nc + nc - 1 - c, cb))
    blk = lambda im: pl.BlockSpec((ML_CHUNK, ML_WIDTH), im)
    sblk = lambda im: pl.BlockSpec((ML_CHUNK, SIDE_W), im)
    nst = 2 * ML_HEADS
    return pl.pallas_call(
        _mlstm_kernel, name="mlstm",
        grid=(B, nc),
        in_specs=[blk(fwd(0)), blk(fwd(1)), blk(fwd(2)), sblk(fwd(0)),
                  blk(bwd(0)), blk(bwd(1)), blk(bwd(2)), sblk(bwd(0))],
        out_specs=[blk(fwd(0)), blk(bwd(0))],
        out_shape=[jax.ShapeDtypeStruct((T, ML_WIDTH), BF16)] * 2,
        scratch_shapes=[pltpu.VMEM((nst, ML_HEAD_DIM, ML_HEAD_DIM), F32),
                        pltpu.VMEM((nst, 1, ML_HEAD_DIM), F32),
                        pltpu.VMEM((nst, 1, 1), F32)],
        compiler_params=_cparams("arbitrary", "arbitrary"),
    )(proj, proj, proj, side, proj, proj, proj, side)


class _FftPlan:
    def __init__(self, L):
        self.L = L
        self.N = 2 * L
        self.N2 = FFT_N2
        self.N1 = self.N // self.N2
        self.N1h = self.N1 // 2
        self.K1 = self.N1h + 1
        self.K1p = -(-self.K1 // SUBLANES) * SUBLANES
        self.PT = self.N2 + FFT_TPAD
        self.P2 = 2 * self.N2 + FFT_KPAD


def _fft_tables(p):
    k1 = jnp.arange(p.K1p, dtype=I32)
    n1 = jnp.arange(p.N1h, dtype=I32)
    n2 = jnp.arange(p.N2, dtype=I32)
    n = p.N2 * n1[None, :] + n2[:, None]
    ph = (k1[None, :, None] * n[:, None, :]) % p.N
    ang = ph.astype(F32) * (2.0 * math.pi / p.N)
    valid = (k1 < p.K1)[None, :, None]
    c = jnp.where(valid, jnp.cos(ang), 0.0)
    s = jnp.where(valid, jnp.sin(ang), 0.0)
    t1 = jnp.concatenate([c, -s], axis=1).astype(BF16)
    wk = jnp.where((k1 == 0) | (k1 == p.N1h), 1.0, 2.0) / p.N
    wk = jnp.where(k1 < p.K1, wk, 0.0)[None, None, :]
    ct = jnp.swapaxes(c, 1, 2) * wk
    st = jnp.swapaxes(s, 1, 2) * wk
    t3 = jnp.concatenate([ct, -st], axis=2).astype(BF16)
    a2 = ((n2[:, None] * n2[None, :]) % p.N2).astype(F32) * (2.0 * math.pi / p.N2)
    c2, s2 = jnp.cos(a2), jnp.sin(a2)
    g2f = jnp.block([[c2, s2], [-s2, c2]]).astype(BF16)
    g2i = jnp.block([[c2, -s2], [s2, c2]]).astype(BF16)
    return t1, t3, g2f, g2i


def _fft_stage1(p, tst_ref, kd_ref, t1_ref):
    def body(n2, carry):
        xs = tst_ref[pl.ds(n2, p.N1h, stride=p.PT), :]
        a = _dot(t1_ref[n2], xs.astype(BF16))
        kd_ref[pl.ds(n2, p.K1p, stride=p.P2), :] = a[:p.K1p]
        kd_ref[pl.ds(p.N2 + n2, p.K1p, stride=p.P2), :] = a[p.K1p:]
        return carry
    lax.fori_loop(0, p.N2, body, 0, unroll=FFT_UNROLL_OUTER)


def _fft_stage3(p, kd_ref, tst_ref, t3_ref):
    def body(n2, carry):
        br = kd_ref[pl.ds(n2, p.K1p, stride=p.P2), :]
        bi = kd_ref[pl.ds(p.N2 + n2, p.K1p, stride=p.P2), :]
        bc = jnp.concatenate([br, bi], axis=0).astype(BF16)
        tst_ref[pl.ds(n2, p.N1h, stride=p.PT), :] = _dot(t3_ref[n2], bc)
        return carry
    lax.fori_loop(0, p.N2, body, 0, unroll=FFT_UNROLL_OUTER)


def _hyfeat_kernel(w1t_ref, w1c_ref, w1s_ref, b1_ref, w2_ref, b2_ref, fr_ref, o_ref, *, L, rows):
    i = pl.program_id(0)
    t = (lax.broadcasted_iota(I32, (rows, 1), 0) + i * rows).astype(F32) / L
    bands = (lax.broadcasted_iota(I32, (1, LANES), 1) + 1).astype(F32)
    ang = (2.0 * math.pi * t) * bands
    z = (t * w1t_ref[...] + _dot(jnp.cos(ang), w1c_ref[...], precision=HI)
         + _dot(jnp.sin(ang), w1s_ref[...], precision=HI) + b1_ref[...])
    h = jnp.sin(fr_ref[0:1, :] * z)
    h = jnp.sin(fr_ref[1:2, :] * (_dot(h, w2_ref[...], precision=HI) + b2_ref[...]))
    o_ref[...] = h


def _hyena_features(L, fw1, fb1, fw2, fb2, freq):
    rows = min(512, L)
    w1c = jnp.zeros((LANES, HY_FFN), F32).at[:HY_BANDS].set(fw1[1:1 + HY_BANDS])
    w1s = jnp.zeros((LANES, HY_FFN), F32).at[:HY_BANDS].set(fw1[1 + HY_BANDS:])
    full = lambda shp: pl.BlockSpec(shp, lambda i: (0,) * len(shp))
    return pl.pallas_call(
        functools.partial(_hyfeat_kernel, L=L, rows=rows), name="hy_feat",
        grid=(L // rows,),
        in_specs=[full((1, HY_FFN)), full((LANES, HY_FFN)), full((LANES, HY_FFN)), full((1, HY_FFN)),
                  full((HY_FFN, HY_FFN)), full((1, HY_FFN)), full((2, HY_FFN))],
        out_specs=pl.BlockSpec((rows, HY_FFN), lambda i: (i, 0)),
        out_shape=jax.ShapeDtypeStruct((L, HY_FFN), F32),
        compiler_params=_cparams("arbitrary"),
    )(fw1[0:1], w1c, w1s, fb1.reshape(1, -1), fw2, fb2.reshape(1, -1), freq)


def _hyfilt_kernel(h2_ref, wf_ref, wb_ref, df_ref, db_ref, skip_ref, t1_ref, g2_ref, o_ref,
                   tsf_ref, tsb_ref, kdf_ref, kdb_ref, *, p):
    L, N2 = p.L, p.N2
    decay_f = jnp.exp(df_ref[...])
    decay_b = jnp.exp(db_ref[...])

    def gen(n1, carry):
        r0 = pl.multiple_of(n1 * N2, N2)
        t = (lax.broadcasted_iota(I32, (N2, 1), 0) + r0).astype(F32) / L
        h2 = h2_ref[pl.ds(r0, N2), :]
        f = _dot(h2, wf_ref[...], precision=HI) * jnp.exp(-t * decay_f)
        b = _dot(h2, wb_ref[...], precision=HI) * jnp.exp(-t * decay_b)
        b = jnp.where(t == 0.0, 0.0, b)
        s0 = pl.multiple_of(n1 * p.PT, SUBLANES)
        tsf_ref[pl.ds(s0, N2), :] = f
        tsb_ref[pl.ds(s0, N2), :] = b
        return carry + jnp.sum(jnp.abs(f) + jnp.abs(b), axis=0, keepdims=True)

    l1 = lax.fori_loop(0, p.N1h, gen, jnp.zeros((1, LANES), F32), unroll=2)
    inv = 1.0 / l1
    _fft_stage1(p, tsf_ref, kdf_ref, t1_ref)
    _fft_stage1(p, tsb_ref, kdb_ref, t1_ref)

    def spec(k1, carry):
        s0 = pl.multiple_of(k1 * p.P2, SUBLANES)
        sf = kdf_ref[pl.ds(s0, 2 * N2), :].astype(BF16)
        sb = kdb_ref[pl.ds(s0, 2 * N2), :].astype(BF16)
        xf = _dot(g2_ref[...], sf)
        xb = _dot(g2_ref[...], sb)
        kr = (xf[:N2] + xb[:N2]) * inv + skip_ref[...]
        ki = (xf[N2:] - xb[N2:]) * inv
        o_ref[k1] = jnp.concatenate([kr, ki], axis=0).astype(BF16)
        return carry
    lax.fori_loop(0, p.K1, spec, 0, unroll=FFT_UNROLL_MID)


def _hyena_spectra(p, h2, fw3, log_decay, skip, t1, g2f):
    nct = HY_WIDTH // LANES
    col = lambda d: (lambda o, c: (0, (2 * o + d) * nct + c))
    full = lambda shp: pl.BlockSpec(shp, lambda o, c: (0,) * len(shp))
    ld = log_decay.reshape(1, -1)
    return pl.pallas_call(
        functools.partial(_hyfilt_kernel, p=p), name="hy_spectra",
        grid=(HY_ORDER, nct),
        in_specs=[full((p.L, HY_FFN)),
                  pl.BlockSpec((HY_FFN, LANES), col(0)), pl.BlockSpec((HY_FFN, LANES), col(1)),
                  pl.BlockSpec((1, LANES), col(0)), pl.BlockSpec((1, LANES), col(1)),
                  pl.BlockSpec((None, 1, LANES), lambda o, c: (o, 0, c)),
                  full(t1.shape), full(g2f.shape)],
        out_specs=pl.BlockSpec((None, p.K1, 2 * p.N2, LANES), lambda o, c: (o, 0, 0, c)),
        out_shape=jax.ShapeDtypeStruct((HY_ORDER, p.K1, 2 * p.N2, HY_WIDTH), BF16),
        scratch_shapes=[pltpu.VMEM((p.N1h * p.PT, LANES), F32), pltpu.VMEM((p.N1h * p.PT, LANES), F32),
                        pltpu.VMEM((p.K1p * p.P2, LANES), F32), pltpu.VMEM((p.K1p * p.P2, LANES), F32)],
        compiler_params=_cparams("arbitrary", "arbitrary"),
    )(h2, fw3, fw3, ld, ld, skip.reshape(HY_ORDER, 1, HY_WIDTH), t1, g2f)


def _conv3_chunk(u_ref, cw_ref, n1, nchunks, L):
    N2 = FFT_N2
    pk = 2 * SUBLANES
    r0 = pl.multiple_of(n1 * N2, N2)
    cur = u_ref[pl.ds(r0, N2), :].astype(F32)
    pstart = pl.multiple_of(jnp.maximum(r0 - pk, 0), pk)
    nstart = pl.multiple_of(jnp.minimum(r0 + N2, L - pk), pk)
    prev = u_ref[pl.ds(pstart, pk), :].astype(F32)[pk - 1:pk, :]
    nxt = u_ref[pl.ds(nstart, pk), :].astype(F32)[0:1, :]
    prev = prev * jnp.where(n1 > 0, 1.0, 0.0)
    nxt = nxt * jnp.where(n1 < nchunks - 1, 1.0, 0.0)
    row = lax.broadcasted_iota(I32, (N2, LANES), 0)
    up = jnp.where(row == 0, prev, pltpu.roll(cur, 1, axis=0))
    dn = jnp.where(row == N2 - 1, nxt, pltpu.roll(cur, N2 - 1, axis=0))
    return cw_ref[0:1, :] * up + cw_ref[1:2, :] * cur + cw_ref[2:3, :] * dn + cw_ref[3:4, :]


def _hyconv_kernel(ua_ref, ub_ref, cwa_ref, cwb_ref, sp_ref, t1_ref, t3_ref, g2f_ref, g2i_ref,
                   o_ref, tst_ref, kd_ref, *, p, pre_a):
    L, N2 = p.L, p.N2

    def chunk_a(n1):
        if pre_a:
            return _conv3_chunk(ua_ref, cwa_ref, n1, p.N1h, L)
        return ua_ref[pl.ds(pl.multiple_of(n1 * N2, N2), N2), :].astype(F32)

    def load(n1, carry):
        tst_ref[pl.ds(pl.multiple_of(n1 * p.PT, SUBLANES), N2), :] = chunk_a(n1)
        return carry
    lax.fori_loop(0, p.N1h, load, 0, unroll=2)

    _fft_stage1(p, tst_ref, kd_ref, t1_ref)

    def mid(k1, carry):
        s0 = pl.multiple_of(k1 * p.P2, SUBLANES)
        x = _dot(g2f_ref[...], kd_ref[pl.ds(s0, 2 * N2), :].astype(BF16))
        xr, xi = x[:N2], x[N2:]
        kk = sp_ref[k1].astype(F32)
        kr, ki = kk[:N2], kk[N2:]
        y = jnp.concatenate([xr * kr - xi * ki, xr * ki + xi * kr], axis=0).astype(BF16)
        kd_ref[pl.ds(s0, 2 * N2), :] = _dot(g2i_ref[...], y)
        return carry
    lax.fori_loop(0, p.K1, mid, 0, unroll=FFT_UNROLL_MID)

    _fft_stage3(p, kd_ref, tst_ref, t3_ref)

    def store(n1, carry):
        y = tst_ref[pl.ds(pl.multiple_of(n1 * p.PT, SUBLANES), N2), :]
        g = _conv3_chunk(ub_ref, cwb_ref, n1, p.N1h, L)
        o_ref[pl.ds(pl.multiple_of(n1 * N2, N2), N2), :] = (g * y).astype(BF16)
        return carry
    lax.fori_loop(0, p.N1h, store, 0, unroll=2)


def _hyconv(p, a_arr, a_col, b_arr, b_col, cw_pack, spectra, order, tables, B, pre_a):
    L = p.L
    T = B * L
    nct = HY_WIDTH // LANES
    t1, t3, g2f, g2i = tables
    full = lambda shp: pl.BlockSpec(shp, lambda c, b: (0,) * len(shp))
    acb, bcb = a_col // LANES, b_col // LANES
    cwa_cb = (a_col - COL_HY) // LANES if pre_a else 0
    cwb_cb = (b_col - COL_HY) // LANES
    return pl.pallas_call(
        functools.partial(_hyconv_kernel, p=p, pre_a=pre_a), name="hy_conv",
        grid=(nct, B),
        in_specs=[pl.BlockSpec((L, LANES), lambda c, b: (b, acb + c)),
                  pl.BlockSpec((L, LANES), lambda c, b: (b, bcb + c)),
                  pl.BlockSpec((SUBLANES, LANES), lambda c, b: (0, cwa_cb + c)),
                  pl.BlockSpec((SUBLANES, LANES), lambda c, b: (0, cwb_cb + c)),
                  pl.BlockSpec((None, p.K1, 2 * p.N2, LANES), lambda c, b: (order, 0, 0, c)),
                  full(t1.shape), full(t3.shape), full(g2f.shape), full(g2i.shape)],
        out_specs=pl.BlockSpec((L, LANES), lambda c, b: (b, c)),
        out_shape=jax.ShapeDtypeStruct((T, HY_WIDTH), BF16),
        scratch_shapes=[pltpu.VMEM((p.N1h * p.PT, LANES), F32), pltpu.VMEM((p.K1p * p.P2, LANES), F32)],
        compiler_params=_cparams("arbitrary", "arbitrary"),
    )(a_arr, b_arr, cw_pack, cw_pack, spectra, t1, t3, g2f, g2i)


def _mlaprep_kernel(cq_ref, ckv_ref, side_ref, cs_ref, gq_ref, gkv_ref, wuq_ref, wukv_ref,
                    q_ref, k_ref, v_ref):
    cqn = _rms(cq_ref[...].astype(F32), gq_ref[...]).astype(BF16)
    ckvn = _rms(ckv_ref[...].astype(F32), gkv_ref[...]).astype(BF16)
    qa = _dot(cqn, wuq_ref[...])
    kv = _dot(ckvn, wukv_ref[...])
    cs = cs_ref[...]
    cos2, sin2 = cs[:, :MLA_ROPE], cs[:, MLA_ROPE:]
    side = side_ref[...]
    k_rope = (side[:, :MLA_ROPE] * cos2 + side[:, MLA_ROPE:2 * MLA_ROPE] * sin2).astype(BF16)
    scale = MLA_QK ** -0.5 * math.log2(math.e)
    hw = MLA_NOPE + 2 * MLA_ROPE
    for h in range(MLA_HEADS):
        b0 = h * hw
        q_rope = qa[:, b0 + MLA_NOPE:b0 + MLA_QK] * cos2 + qa[:, b0 + MLA_QK:b0 + hw] * sin2
        q_ref[h, :, :MLA_NOPE] = (qa[:, b0:b0 + MLA_NOPE] * scale).astype(BF16)
        q_ref[h, :, MLA_NOPE:] = (q_rope * scale).astype(BF16)
        k_ref[h, :, :MLA_NOPE] = kv[:, b0:b0 + MLA_NOPE].astype(BF16)
        k_ref[h, :, MLA_NOPE:] = k_rope
        v_ref[h] = kv[:, b0 + MLA_NOPE:b0 + MLA_NOPE + MLA_V].astype(BF16)


def _mla_prep(proj, side, cs, gq, gkv, wuq_ext, wukv, B, L):
    T = B * L
    tm = min(512, L)
    nb = L // tm
    hw = MLA_NOPE + 2 * MLA_ROPE
    full = lambda shp: pl.BlockSpec(shp, lambda i: (0,) * len(shp))
    oidx = lambda i: (i // nb, 0, i % nb, 0)
    return pl.pallas_call(
        _mlaprep_kernel, name="mla_prep",
        grid=(T // tm,),
        in_specs=[pl.BlockSpec((tm, MLA_Q_RANK), lambda i: (i, COL_CQ // MLA_Q_RANK)),
                  pl.BlockSpec((tm, MLA_KV_RANK), lambda i: (i, COL_CKV // MLA_KV_RANK)),
                  pl.BlockSpec((tm, SIDE_W), lambda i: (i, 0)),
                  pl.BlockSpec((tm, 2 * MLA_ROPE), lambda i: (i % nb, 0)),
                  full((1, MLA_Q_RANK)), full((1, MLA_KV_RANK)),
                  full((MLA_Q_RANK, MLA_HEADS * hw)), full((MLA_KV_RANK, MLA_HEADS * (MLA_NOPE + MLA_V)))],
        out_specs=[pl.BlockSpec((None, MLA_HEADS, tm, MLA_QK), oidx),
                   pl.BlockSpec((None, MLA_HEADS, tm, MLA_QK), oidx),
                   pl.BlockSpec((None, MLA_HEADS, tm, MLA_V), oidx)],
        out_shape=[jax.ShapeDtypeStruct((B, MLA_HEADS, L, MLA_QK), BF16),
                   jax.ShapeDtypeStruct((B, MLA_HEADS, L, MLA_QK), BF16),
                   jax.ShapeDtypeStruct((B, MLA_HEADS, L, MLA_V), BF16)],
        compiler_params=_cparams("arbitrary"),
    )(proj, proj, side, cs, gq, gkv, wuq_ext, wukv)


def _flash_kernel(q_ref, k_ref, v_ref, o_ref, *, tq, tk, nsplit):
    L = k_ref.shape[0]
    sub = tq // nsplit
    qs = [q_ref[pl.ds(i * sub, sub), :] for i in range(nsplit)]

    def step(j, carry):
        k0 = pl.multiple_of(j * tk, tk)
        kj = k_ref[pl.ds(k0, tk), :]
        vj = v_ref[pl.ds(k0, tk), :]
        out = []
        for i in range(nsplit):
            m_prev, l_prev, acc = carry[i]
            s = _dot_nt(qs[i], kj)
            m_new = jnp.maximum(m_prev, jnp.max(s, axis=-1, keepdims=True))
            alpha = jnp.exp2(m_prev - m_new)
            pr = jnp.exp2(s - m_new)
            l_new = alpha * l_prev + jnp.sum(pr, axis=-1, keepdims=True)
            acc = alpha * acc + _dot(pr.astype(BF16), vj)
            out.append((m_new, l_new, acc))
        return tuple(out)

    init = tuple((jnp.full((sub, 1), -jnp.inf, F32), jnp.zeros((sub, 1), F32), jnp.zeros((sub, MLA_V), F32))
                 for _ in range(nsplit))
    res = lax.fori_loop(0, L // tk, step, init)
    for i in range(nsplit):
        _, l_fin, acc = res[i]
        o_ref[pl.ds(i * sub, sub), :] = (acc / l_fin).astype(BF16)


def _flash(q, k, v, B, L):
    tq = min(ATTN_TQ, L)
    tk = min(ATTN_TK, L)
    nq = L // tq
    return pl.pallas_call(
        functools.partial(_flash_kernel, tq=tq, tk=tk, nsplit=ATTN_SPLIT), name="mla_attn",
        grid=(B, MLA_HEADS, nq),
        in_specs=[pl.BlockSpec((None, None, tq, MLA_QK), lambda b, h, i: (b, h, i, 0)),
                  pl.BlockSpec((None, None, L, MLA_QK), lambda b, h, i: (b, h, 0, 0)),
                  pl.BlockSpec((None, None, L, MLA_V), lambda b, h, i: (b, h, 0, 0))],
        out_specs=pl.BlockSpec((tq, MLA_V), lambda b, h, i: (b * nq + i, h)),
        out_shape=jax.ShapeDtypeStruct((B * L, MLA_WIDTH), BF16),
        compiler_params=_cparams("arbitrary", "arbitrary", "arbitrary"),
    )(q, k, v)


def _merge_kernel(hf_ref, hb_ref, mlo_ref, mlg_ref, ohy_ref, omla_ref, gml_ref, ghy_ref, gmla_ref,
                  wml_ref, why_ref, wmla_ref, o_ref, oml_ref):
    @pl.when(pl.program_id(1) == 0)
    def _():
        for h in range(ML_HEADS):
            hs = slice(h * ML_HEAD_DIM, (h + 1) * ML_HEAD_DIM)
            x = hf_ref[:, hs].astype(F32) + hb_ref[:, hs].astype(F32)
            y = _rms(x, mlg_ref[:, hs])
            oml_ref[:, hs] = (y * _sigmoid(mlo_ref[:, hs].astype(F32))).astype(BF16)

    acc = _sigmoid(gml_ref[...].astype(F32)) * _dot(oml_ref[...], wml_ref[...])
    acc += _sigmoid(ghy_ref[...].astype(F32)) * _dot(ohy_ref[...], why_ref[...])
    acc += _sigmoid(gmla_ref[...].astype(F32)) * _dot(omla_ref[...], wmla_ref[...])
    o_ref[...] = acc.astype(BF16)


def _merge(hf, hb, proj, ml_norm_g, o_hy, o_mla, wml, why, wmla, L):
    T = hf.shape[0]
    tm = min(1024, L)
    tn = 512
    row = lambda cb: (lambda i, j: (i, cb))
    gate = lambda k: (lambda i, j: (i, (COL_BRG + k * D_MODEL) // tn + j))
    wsp = pl.BlockSpec((ML_WIDTH, tn), lambda i, j: (0, j))
    return pl.pallas_call(
        _merge_kernel, name="merge",
        grid=(T // tm, D_MODEL // tn),
        in_specs=[pl.BlockSpec((tm, ML_WIDTH), row(0)), pl.BlockSpec((tm, ML_WIDTH), row(0)),
                  pl.BlockSpec((tm, ML_WIDTH), row(3)),
                  pl.BlockSpec((1, ML_WIDTH), lambda i, j: (0, 0)),
                  pl.BlockSpec((tm, HY_WIDTH), row(0)), pl.BlockSpec((tm, MLA_WIDTH), row(0)),
                  pl.BlockSpec((tm, tn), gate(0)), pl.BlockSpec((tm, tn), gate(1)), pl.BlockSpec((tm, tn), gate(2)),
                  wsp, wsp, wsp],
        out_specs=pl.BlockSpec((tm, tn), lambda i, j: (i, j)),
        out_shape=jax.ShapeDtypeStruct((T, D_MODEL), BF16),
        scratch_shapes=[pltpu.VMEM((tm, ML_WIDTH), BF16)],
        compiler_params=_cparams("arbitrary", "arbitrary"),
    )(hf, hb, proj, ml_norm_g, o_hy, o_mla, proj, proj, proj, wml, why, wmla)


def _outproj_kernel(m_ref, w_ref, x_ref, g_ref, o_ref):
    o_ref[...] = x_ref[...] + g_ref[0] * _dot(m_ref[...], w_ref[...])


def _out_proj(merged, w_out, x2d, g1, L):
    T = x2d.shape[0]
    tm = min(1024, L)
    tn = 512
    return pl.pallas_call(
        _outproj_kernel, name="out_proj",
        grid=(T // tm, D_MODEL // tn),
        in_specs=[pl.BlockSpec((tm, D_MODEL), lambda i, j: (i, 0)),
                  pl.BlockSpec((D_MODEL, tn), lambda i, j: (0, j)),
                  pl.BlockSpec((tm, tn), lambda i, j: (i, j)),
                  pl.BlockSpec((1, 1, tn), lambda i, j: ((i * tm) // L, 0, j))],
        out_specs=pl.BlockSpec((tm, tn), lambda i, j: (i, j)),
        out_shape=jax.ShapeDtypeStruct((T, D_MODEL), F32),
        compiler_params=_cparams("arbitrary", "arbitrary"),
    )(merged, w_out, x2d, g1)


def _pack_bf16_pair(lo, hi):
    lb = lax.bitcast_convert_type(lo.astype(BF16).astype(F32), U32)
    hb = lax.bitcast_convert_type(hi.astype(BF16).astype(F32), U32)
    return hb | (lb >> 16)


def _unpack_bf16_pair(w):
    lo = lax.bitcast_convert_type(w << 16, F32)
    hi = lax.bitcast_convert_type(w & jnp.uint32(0xFFFF0000), F32)
    return lo, hi


def _route_kernel(x_ref, g_ref, sc_ref, sh_ref, wr_ref, rb_ref, hp_ref, idx_ref, wgt_ref):
    half = D_MODEL // 2
    hn = _rms(x_ref[...], g_ref[...]) * (1.0 + sc_ref[0]) + sh_ref[0]
    hp_ref[...] = _pack_bf16_pair(hn[:, :half], hn[:, half:])
    logits = _dot_nt(wr_ref[...], hn, precision=HI)
    scores = _sigmoid(logits)
    sel = scores + rb_ref[...]
    rows = [sel[e:e + 1, :] for e in range(N_EXPERTS)]
    srow = [scores[e:e + 1, :] for e in range(N_EXPERTS)]
    gs = []
    for g in range(N_GROUPS):
        r = rows[g * EXPERTS_PER_GROUP:(g + 1) * EXPERTS_PER_GROUP]
        best = None
        for a in range(EXPERTS_PER_GROUP):
            for b in range(a + 1, EXPERTS_PER_GROUP):
                s = r[a] + r[b]
                best = s if best is None else jnp.maximum(best, s)
        gs.append(best)
    gsel = jnp.zeros_like(gs[0], dtype=I32)
    gbest = gs[0]
    for g in range(1, N_GROUPS):
        better = gs[g] > gbest
        gsel = jnp.where(better, g, gsel)
        gbest = jnp.where(better, gs[g], gbest)
    masked = [jnp.where(gsel == (e // EXPERTS_PER_GROUP), rows[e], -jnp.inf) for e in range(N_EXPERTS)]
    picks = []
    for _ in range(2):
        bi = jnp.zeros_like(gsel)
        bv = masked[0]
        bs = srow[0]
        for e in range(1, N_EXPERTS):
            better = masked[e] > bv
            bi = jnp.where(better, e, bi)
            bv = jnp.where(better, masked[e], bv)
            bs = jnp.where(better, srow[e], bs)
        picks.append((bi, bs))
        masked = [jnp.where(bi == e, -jnp.inf, masked[e]) for e in range(N_EXPERTS)]
    (i1, s1), (i2, s2) = picks
    tot = s1 + s2
    idx_ref[...] = jnp.concatenate([i1, i2], axis=0)
    wgt_ref[...] = jnp.concatenate([s1 / tot, s2 / tot], axis=0)


def _route(x2d, g, sc, sh, w_router, router_bias, L):
    T = x2d.shape[0]
    tm = min(512, L)
    bidx = lambda i: ((i * tm) // L, 0, 0)
    return pl.pallas_call(
        _route_kernel, name="route",
        grid=(T // tm,),
        in_specs=[pl.BlockSpec((tm, D_MODEL), lambda i: (i, 0)),
                  pl.BlockSpec((1, D_MODEL), lambda i: (0, 0)),
                  pl.BlockSpec((1, 1, D_MODEL), bidx), pl.BlockSpec((1, 1, D_MODEL), bidx),
                  pl.BlockSpec((N_EXPERTS, D_MODEL), lambda i: (0, 0)),
                  pl.BlockSpec((N_EXPERTS, 1), lambda i: (0, 0))],
        out_specs=[pl.BlockSpec((tm, D_MODEL // 2), lambda i: (i, 0)),
                   pl.BlockSpec((2, tm), lambda i: (0, i)),
                   pl.BlockSpec((2, tm), lambda i: (0, i))],
        out_shape=[jax.ShapeDtypeStruct((T, D_MODEL // 2), U32),
                   jax.ShapeDtypeStruct((2, T), I32),
                   jax.ShapeDtypeStruct((2, T), F32)],
        compiler_params=_cparams("arbitrary"),
    )(x2d, g, sc, sh, w_router.T, router_bias.reshape(N_EXPERTS, 1))


def _moe_plan(idx, T):
    A = 2 * T
    e = idx.reshape(A)
    tok = jnp.tile(jnp.arange(T, dtype=I32), 2)
    onehot = (e[:, None] == jnp.arange(N_EXPERTS, dtype=I32)[None, :]).astype(I32)
    csum = jnp.cumsum(onehot, axis=0)
    counts = csum[-1]
    rank = jnp.sum(csum * onehot, axis=1) - 1
    padded = ((counts + MOE_TILE - 1) // MOE_TILE) * MOE_TILE
    ends = jnp.cumsum(padded)
    starts = ends - padded
    pos = jnp.sum(starts[None, :] * onehot, axis=1) + rank
    R = A + N_EXPERTS * MOE_TILE
    src = jnp.zeros((R,), I32).at[pos].set(tok)
    n_tiles = R // MOE_TILE
    tile_start = jnp.arange(n_tiles, dtype=I32) * MOE_TILE
    tile_e = jnp.minimum(jnp.sum((tile_start[:, None] >= ends[None, :]).astype(I32), axis=1), N_EXPERTS - 1)
    num_used = (ends[-1] // MOE_TILE).astype(I32).reshape(1)
    return src, pos.reshape(2, T).astype(I32), tile_e.astype(I32), num_used


def _row_gather_copy(src_hbm, dst_ref, sem, src_row, dst_row):
    return pltpu.make_async_copy(src_hbm.at[pl.ds(src_row, 1)], dst_ref.at[pl.ds(dst_row, 1)], sem)


def _issue_row_gathers(src_hbm, idx_ref, dst_ref, sem, rows, dst_base=0):
    for r in range(rows):
        _row_gather_copy(src_hbm, dst_ref, sem, idx_ref[0, 0, r], dst_base + r).start()


def _wait_row_gathers(src_hbm, dst_ref, sem):
    pltpu.make_async_copy(src_hbm.at[pl.ds(0, dst_ref.shape[0])], dst_ref, sem).wait()


def _expert_kernel(te_ref, nt_ref, idx_ref, idxn_ref, hp_hbm, wgu_ref, wd_ref, o_ref, xbuf_ref, sems):
    t = pl.program_id(0)
    nt = nt_ref[0]
    slot = lax.rem(t, 2)
    half = D_MODEL // 2

    @pl.when(t == 0)
    def _():
        _issue_row_gathers(hp_hbm, idx_ref, xbuf_ref.at[0], sems.at[0], MOE_TILE)

    @pl.when(t < nt)
    def _():
        _wait_row_gathers(hp_hbm, xbuf_ref.at[slot], sems.at[slot])
        _issue_row_gathers(hp_hbm, idxn_ref, xbuf_ref.at[1 - slot], sems.at[1 - slot], MOE_TILE)
        lo, hi = _unpack_bf16_pair(xbuf_ref[slot])
        h1 = _dot(lo.astype(BF16), wgu_ref[:half, :]) + _dot(hi.astype(BF16), wgu_ref[half:, :])
        a, b = h1[:, :D_EXPERT], h1[:, D_EXPERT:]
        act = (a * _sigmoid(a) * b).astype(BF16)
        y = _dot(act, wd_ref[...])
        o_ref[...] = _pack_bf16_pair(y[:, :half], y[:, half:])

    @pl.when(t == nt - 1)
    def _():
        _wait_row_gathers(hp_hbm, xbuf_ref.at[1 - slot], sems.at[1 - slot])

    @pl.when(t >= nt)
    def _():
        o_ref[...] = jnp.zeros_like(o_ref)


def _experts(hp, src, tile_e, num_used, wgu, wd, l):
    R = src.shape[0]
    nt = R // MOE_TILE
    half = D_MODEL // 2
    src3 = src.reshape(nt, 1, MOE_TILE)
    grid_spec = pltpu.PrefetchScalarGridSpec(
        num_scalar_prefetch=2,
        grid=(nt,),
        in_specs=[pl.BlockSpec((1, 1, MOE_TILE), lambda t, te, n: (t, 0, 0), memory_space=pltpu.SMEM),
                  pl.BlockSpec((1, 1, MOE_TILE), lambda t, te, n: (jnp.minimum(t + 1, n[0] - 1), 0, 0),
                               memory_space=pltpu.SMEM),
                  pl.BlockSpec(memory_space=pl.ANY),
                  pl.BlockSpec((None, None, D_MODEL, 2 * D_EXPERT), lambda t, te, n: (l, te[t], 0, 0)),
                  pl.BlockSpec((None, None, D_EXPERT, D_MODEL), lambda t, te, n: (l, te[t], 0, 0))],
        out_specs=pl.BlockSpec((MOE_TILE, half), lambda t, te, n: (t, 0)),
        scratch_shapes=[pltpu.VMEM((2, MOE_TILE, half), U32), pltpu.SemaphoreType.DMA((2,))],
    )
    return pl.pallas_call(
        _expert_kernel, name="moe_experts",
        grid_spec=grid_spec,
        out_shape=jax.ShapeDtypeStruct((R, half), U32),
        compiler_params=_cparams("arbitrary"),
    )(tile_e, num_used, src3, src3, hp, wgu, wd)


def _combine_kernel(p1_ref, p2_ref, p1n_ref, p2n_ref, ys_hbm, w_ref, x_ref, g_ref, fg_ref, o_ref, buf_ref, sems,
                    *, rows, final):
    i = pl.program_id(0)
    n = pl.num_programs(0)
    slot = lax.rem(i, 2)
    half = D_MODEL // 2

    @pl.when(i == 0)
    def _():
        _issue_row_gathers(ys_hbm, p1_ref, buf_ref.at[0], sems.at[0], rows)
        _issue_row_gathers(ys_hbm, p2_ref, buf_ref.at[0], sems.at[0], rows, rows)

    _wait_row_gathers(ys_hbm, buf_ref.at[slot], sems.at[slot])
    _issue_row_gathers(ys_hbm, p1n_ref, buf_ref.at[1 - slot], sems.at[1 - slot], rows)
    _issue_row_gathers(ys_hbm, p2n_ref, buf_ref.at[1 - slot], sems.at[1 - slot], rows, rows)

    w = w_ref[...]
    w1, w2 = w[:, 0:1], w[:, 1:2]
    lo1, hi1 = _unpack_bf16_pair(buf_ref[slot, pl.ds(0, rows), :])
    lo2, hi2 = _unpack_bf16_pair(buf_ref[slot, pl.ds(rows, rows), :])
    g = g_ref[0]
    xlo = x_ref[:, :half] + g[:, :half] * (w1 * lo1 + w2 * lo2)
    xhi = x_ref[:, half:] + g[:, half:] * (w1 * hi1 + w2 * hi2)
    if final:
        ms = (jnp.sum(xlo * xlo, axis=-1, keepdims=True) + jnp.sum(xhi * xhi, axis=-1, keepdims=True)) / D_MODEL
        r = lax.rsqrt(ms + EPS)
        xlo = xlo * r * fg_ref[:, :half]
        xhi = xhi * r * fg_ref[:, half:]
    o_ref[:, :half] = xlo
    o_ref[:, half:] = xhi

    @pl.when(i == n - 1)
    def _():
        _wait_row_gathers(ys_hbm, buf_ref.at[1 - slot], sems.at[1 - slot])


def _combine(ys, pos, wgt, x2d, g2, final_g, L, final):
    T = x2d.shape[0]
    rows = GATHER_TILE
    nt = T // rows
    half = D_MODEL // 2
    cur = lambda: pl.BlockSpec((1, 1, rows), lambda i: (i, 0, 0), memory_space=pltpu.SMEM)
    nxt = lambda: pl.BlockSpec((1, 1, rows), lambda i: (jnp.minimum(i + 1, nt - 1), 0, 0), memory_space=pltpu.SMEM)
    p1 = pos[0].reshape(nt, 1, rows)
    p2 = pos[1].reshape(nt, 1, rows)
    return pl.pallas_call(
        functools.partial(_combine_kernel, rows=rows, final=final), name="moe_combine",
        grid=(nt,),
        in_specs=[cur(), cur(), nxt(), nxt(),
                  pl.BlockSpec(memory_space=pl.ANY),
                  pl.BlockSpec((rows, 2), lambda i: (i, 0)),
                  pl.BlockSpec((rows, D_MODEL), lambda i: (i, 0)),
                  pl.BlockSpec((1, 1, D_MODEL), lambda i: ((i * rows) // L, 0, 0)),
                  pl.BlockSpec((1, D_MODEL), lambda i: (0, 0))],
        out_specs=pl.BlockSpec((rows, D_MODEL), lambda i: (i, 0)),
        out_shape=jax.ShapeDtypeStruct((T, D_MODEL), F32),
        scratch_shapes=[pltpu.VMEM((2, 2 * rows, half), U32), pltpu.SemaphoreType.DMA((2,))],
        compiler_params=_cparams("arbitrary"),
    )(p1, p2, p1, p2, ys, wgt.T, x2d, g2, final_g)


def _swap_halves(w):
    h = w.shape[-1] // 2
    return jnp.concatenate([w[..., h:], w[..., :h]], axis=-1)


def _prep_layer(P, l):
    w = P["w_in"][l]
    o_g = 4 * ML_WIDTH
    o_hy = o_g + 4 * ML_HEADS
    o_cq = o_hy + 3 * HY_WIDTH
    o_ckv = o_cq + MLA_Q_RANK
    o_kr = o_ckv + MLA_KV_RANK
    o_br = o_kr + MLA_ROPE
    kr = w[:, o_kr:o_br]
    side_pad = jnp.zeros((D_MODEL, SIDE_W - SIDE_GATE - 4 * ML_HEADS), F32)
    w_in_r = jnp.concatenate(
        [w[:, :o_g], w[:, o_hy:o_cq], w[:, o_br:], w[:, o_cq:o_ckv], w[:, o_ckv:o_kr],
         kr, _swap_halves(kr), w[:, o_g:o_hy], side_pad], axis=1).astype(BF16)
    side_b = jnp.zeros((1, SIDE_W), F32).at[0, SIDE_GATE:SIDE_GATE + 4 * ML_HEADS].set(P["ml_gate_b"][l].reshape(-1))
    wuq = P["w_uq"][l].reshape(MLA_Q_RANK, MLA_HEADS, MLA_QK)
    qr = wuq[..., MLA_NOPE:]
    wuq_ext = jnp.concatenate([wuq, _swap_halves(qr)], axis=-1).reshape(MLA_Q_RANK, -1).astype(BF16)
    cw = jnp.concatenate([P["hy_conv_w"][l], P["hy_conv_b"][l][None, :],
                          jnp.zeros((SUBLANES - 4, 3 * HY_WIDTH), F32)], axis=0)
    return dict(
        w_in_r=w_in_r, side_b=side_b, wuq_ext=wuq_ext, wukv=P["w_ukv"][l].astype(BF16), cw=cw,
        wml=P["w_br_ml"][l].astype(BF16), why=P["w_br_hy"][l].astype(BF16), wmla=P["w_br_mla"][l].astype(BF16),
        w_out=P["w_out"][l].astype(BF16),
        norm1_g=P["norm1_g"][l].reshape(1, -1), norm2_g=P["norm2_g"][l].reshape(1, -1),
        ml_norm_g=P["ml_norm_g"][l].reshape(1, -1),
        gq=P["mla_q_norm_g"][l].reshape(1, -1), gkv=P["mla_kv_norm_g"][l].reshape(1, -1),
    )


def _rope_table(L):
    pos = jnp.arange(L, dtype=F32)
    inv = ROPE_BASE ** (-jnp.arange(0, MLA_ROPE, 2, dtype=F32) / MLA_ROPE)
    ang = pos[:, None] * inv[None, :]
    cos, sin = jnp.cos(ang), jnp.sin(ang)
    return jnp.concatenate([cos, cos, -sin, sin], axis=-1)


def _encoder(x, mods, P, W, wgu_bf, wd_bf):
    B, L, _ = x.shape
    T = B * L
    x2d = x.reshape(T, D_MODEL)
    plan = _FftPlan(L)
    tables = _fft_tables(plan)
    cs = _rope_table(L)
    depth = len(W)
    for l in range(depth):
        wl = W[l]
        sh1, sc1, g1, sh2, sc2, g2 = [mods[l][:, k].reshape(B, 1, D_MODEL) for k in range(6)]
        proj, side = _in_proj(x2d, wl["norm1_g"], sc1, sh1, wl["w_in_r"], wl["side_b"], L)
        hf, hb = _mlstm(proj, side, B, L)
        h2 = _hyena_features(L, P["hy_fw1"][l], P["hy_fb1"][l], P["hy_fw2"][l], P["hy_fb2"][l], P["hy_freq"][l])
        spectra = _hyena_spectra(plan, h2, P["hy_fw3"][l], P["hy_log_decay"][l], P["hy_skip"][l], tables[0], tables[2])
        z = _hyconv(plan, proj, COL_HY, proj, COL_HY + HY_WIDTH, wl["cw"], spectra, 0, tables, B, True)
        o_hy = _hyconv(plan, z, 0, proj, COL_HY + 2 * HY_WIDTH, wl["cw"], spectra, 1, tables, B, False)
        q, k, v = _mla_prep(proj, side, cs, wl["gq"], wl["gkv"], wl["wuq_ext"], wl["wukv"], B, L)
        o_mla = _flash(q, k, v, B, L)
        merged = _merge(hf, hb, proj, wl["ml_norm_g"], o_hy, o_mla, wl["wml"], wl["why"], wl["wmla"], L)
        x2d = _out_proj(merged, wl["w_out"], x2d, g1, L)
        hp, idx, wgt = _route(x2d, wl["norm2_g"], sc2, sh2, P["w_router"], P["router_bias"], L)
        src, pos, tile_e, num_used = _moe_plan(idx, T)
        ys = _experts(hp, src, tile_e, num_used, wgu_bf, wd_bf, l)
        x2d = _combine(ys, pos, wgt, x2d, g2, P["final_g"].reshape(1, -1), L, final=(l == depth - 1))
    return x2d.reshape(B, L, D_MODEL)


def kernel(x_prompt, x_sample, c_prompt, c_sample, w_ada, b_ada, norm1_g, norm2_g, w_in, ml_gate_b, ml_norm_g,
           hy_conv_w, hy_conv_b, hy_fw1, hy_fb1, hy_fw2, hy_fb2, hy_fw3, hy_freq, hy_log_decay, hy_skip,
           mla_q_norm_g, mla_kv_norm_g, w_uq, w_ukv, w_br_ml, w_br_hy, w_br_mla, w_out, w_router, router_bias,
           w_gate_up, w_down, final_g):
    P = dict(w_ada=w_ada, b_ada=b_ada, norm1_g=norm1_g, norm2_g=norm2_g, w_in=w_in, ml_gate_b=ml_gate_b,
             ml_norm_g=ml_norm_g, hy_conv_w=hy_conv_w, hy_conv_b=hy_conv_b, hy_fw1=hy_fw1, hy_fb1=hy_fb1,
             hy_fw2=hy_fw2, hy_fb2=hy_fb2, hy_fw3=hy_fw3, hy_freq=hy_freq, hy_log_decay=hy_log_decay,
             hy_skip=hy_skip, mla_q_norm_g=mla_q_norm_g, mla_kv_norm_g=mla_kv_norm_g, w_uq=w_uq, w_ukv=w_ukv,
             w_br_ml=w_br_ml, w_br_hy=w_br_hy, w_br_mla=w_br_mla, w_out=w_out, w_router=w_router,
             router_bias=router_bias, w_gate_up=w_gate_up, w_down=w_down, final_g=final_g)
    depth = w_in.shape[0]
    W = [_prep_layer(P, l) for l in range(depth)]
    wgu_bf = w_gate_up.astype(BF16)
    wd_bf = w_down.astype(BF16)
    bp, bs = c_prompt.shape[0], c_sample.shape[0]
    rows = -(-(bp + bs) // SUBLANES) * SUBLANES
    c_all = jnp.zeros((rows, D_MODEL), F32).at[:bp].set(c_prompt).at[bp:bp + bs].set(c_sample)
    mods_p, mods_s = [], []
    for l in range(depth):
        mod = _ada(c_all, w_ada, b_ada, l).reshape(rows, 6, D_MODEL)
        mods_p.append(mod[:bp])
        mods_s.append(mod[bp:bp + bs])
    y_prompt = _encoder(x_prompt, mods_p, P, W, wgu_bf, wd_bf)
    y_sample = _encoder(x_sample, mods_s, P, W, wgu_bf, wd_bf)
    return (y_prompt, y_sample)
```

```python
import functools
import math

import jax
import jax.numpy as jnp
from jax import lax
from jax.experimental import pallas as pl
from jax.experimental.pallas import tpu as pltpu

F32 = jnp.float32
BF16 = jnp.bfloat16
U32 = jnp.uint32
I32 = jnp.int32

D_MODEL = 2048
ML_HEADS = 4
ML_HEAD_DIM = 256
ML_WIDTH = ML_HEADS * ML_HEAD_DIM
ML_CHUNK = 128
HY_WIDTH = 1024
HY_ORDER = 2
HY_BANDS = 16
HY_FFN = 64
MLA_HEADS = 8
MLA_Q_RANK = 512
MLA_KV_RANK = 256
MLA_NOPE = 128
MLA_ROPE = 64
MLA_V = 128
MLA_WIDTH = MLA_HEADS * MLA_V
MLA_QK = MLA_NOPE + MLA_ROPE
ROPE_BASE = 10000.0
N_EXPERTS = 16
N_GROUPS = 4
EXPERTS_PER_GROUP = N_EXPERTS // N_GROUPS
D_EXPERT = 1024
EPS = 1e-6

LANES = 128
SUBLANES = 8
VMEM_LIMIT = 56 * 1024 * 1024

COL_ML = 0
COL_HY = 4 * ML_WIDTH
COL_BRG = COL_HY + 3 * HY_WIDTH
COL_CQ = COL_BRG + 3 * D_MODEL
COL_CKV = COL_CQ + MLA_Q_RANK
COL_SIDE = COL_CKV + MLA_KV_RANK
SIDE_W = 256
SIDE_GATE = 128
IN_COLS_R = COL_SIDE + SIDE_W
IN_TN = 512

FFT_N2 = 128
FFT_TPAD = SUBLANES
FFT_KPAD = SUBLANES
FFT_UNROLL_OUTER = 16
FFT_UNROLL_MID = 8
HY_CT = 2 * LANES

MOE_TILE = 256
GATHER_TILE = 256
ATTN_TQ = 2048
ATTN_TK = 2048
ATTN_SPLIT = 2


def _cparams(*sem):
    return pltpu.CompilerParams(dimension_semantics=sem, vmem_limit_bytes=VMEM_LIMIT)


def _rms(x, g):
    return x * lax.rsqrt(jnp.mean(x * x, axis=-1, keepdims=True) + EPS) * g


def _log_sigmoid(x):
    return -(jnp.maximum(-x, 0.0) + jnp.log1p(jnp.exp(-jnp.abs(x))))


def _sigmoid(x):
    return 1.0 / (1.0 + jnp.exp(-x))


def _dot(a, b, **kw):
    return jnp.dot(a, b, preferred_element_type=F32, **kw)


def _dot_nt(a, b, **kw):
    return lax.dot_general(a, b, (((1,), (1,)), ((), ())), preferred_element_type=F32, **kw)


HI = lax.Precision.HIGHEST


def _ada_kernel(c_ref, w_ref, b_ref, o_ref):
    c = c_ref[...]
    s = c * _sigmoid(c)
    o_ref[...] = _dot(s.astype(BF16), w_ref[...].astype(BF16)) + b_ref[...]


def _ada(c_all, w_ada, b_ada, l):
    bp = c_all.shape[0]
    n = w_ada.shape[-1]
    tn = 1024
    return pl.pallas_call(
        _ada_kernel, name="ada_mod",
        grid=(n // tn,),
        in_specs=[
            pl.BlockSpec((bp, D_MODEL), lambda j: (0, 0)),
            pl.BlockSpec((None, D_MODEL, tn), lambda j: (l, 0, j)),
            pl.BlockSpec((None, 1, tn), lambda j: (l, 0, j)),
        ],
        out_specs=pl.BlockSpec((bp, tn), lambda j: (0, j)),
        out_shape=jax.ShapeDtypeStruct((bp, n), F32),
        compiler_params=_cparams("arbitrary"),
    )(c_all, w_ada, b_ada.reshape(b_ada.shape[0], 1, n))


def _in_kernel(x_ref, g_ref, sc_ref, sh_ref, w_ref, sb_ref, o_ref, side_ref, hn_ref, *, nj):
    j = pl.program_id(1)

    @pl.when(j == 0)
    def _():
        y = _rms(x_ref[...], g_ref[...])
        hn_ref[...] = (y * (1.0 + sc_ref[0]) + sh_ref[0]).astype(BF16)

    acc = _dot(hn_ref[...], w_ref[...])
    o_ref[...] = acc.astype(BF16)

    @pl.when(j == nj - 1)
    def _():
        side_ref[...] = acc[:, IN_TN - SIDE_W:] + sb_ref[...]


def _in_proj(x2d, g, sc, sh, w_in_r, side_b, L):
    T = x2d.shape[0]
    tm = min(1024, L)
    nj = IN_COLS_R // IN_TN
    bidx = lambda i, j: ((i * tm) // L, 0, 0)
    return pl.pallas_call(
        functools.partial(_in_kernel, nj=nj), name="in_proj",
        grid=(T // tm, nj),
        in_specs=[
            pl.BlockSpec((tm, D_MODEL), lambda i, j: (i, 0)),
            pl.BlockSpec((1, D_MODEL), lambda i, j: (0, 0)),
            pl.BlockSpec((1, 1, D_MODEL), bidx),
            pl.BlockSpec((1, 1, D_MODEL), bidx),
            pl.BlockSpec((D_MODEL, IN_TN), lambda i, j: (0, j)),
            pl.BlockSpec((1, SIDE_W), lambda i, j: (0, 0)),
        ],
        out_specs=[
            pl.BlockSpec((tm, IN_TN), lambda i, j: (i, j)),
            pl.BlockSpec((tm, SIDE_W), lambda i, j: (i, 0)),
        ],
        out_shape=[
            jax.ShapeDtypeStruct((T, IN_COLS_R), BF16),
            jax.ShapeDtypeStruct((T, SIDE_W), F32),
        ],
        scratch_shapes=[pltpu.VMEM((tm, D_MODEL), BF16)],
        compiler_params=_cparams("arbitrary", "arbitrary"),
    )(x2d, g, sc, sh, w_in_r, side_b)


def _mlstm_chunk(q, k, v, li_col, lf_col, b_col, li_row, b_row, mask, C_ref, n_ref, m_ref, idx):
    m = m_ref[idx]
    C = C_ref[idx]
    n = n_ref[idx]
    Dm = jnp.where(mask, b_col - b_row + li_row, -jnp.inf)
    inter = b_col + m
    m_row = jnp.maximum(inter, jnp.max(Dm, axis=-1, keepdims=True))
    w_inter = jnp.exp(inter - m_row)
    s = _dot_nt(q, k) * jnp.exp(Dm - m_row)
    num = w_inter * _dot(q, C.astype(BF16)) + _dot(s.astype(BF16), v)
    qn = jnp.sum(q.astype(F32) * n, axis=-1, keepdims=True)
    den = w_inter * qn + jnp.sum(s, axis=-1, keepdims=True)
    h = num / jnp.maximum(jnp.abs(den), jnp.exp(-m_row))
    bL = jnp.sum(lf_col, axis=0, keepdims=True)
    g = bL - b_col + li_col
    m_new = jnp.maximum(bL + m, jnp.max(g, axis=0, keepdims=True))
    a = jnp.exp(bL + m - m_new)
    kw = k.astype(F32) * jnp.exp(g - m_new)
    C_ref[idx] = a * C + _dot(kw.T.astype(BF16), v)
    n_ref[idx] = a * n + jnp.sum(kw, axis=0, keepdims=True)
    m_ref[idx] = m_new
    return h


def _mlstm_kernel(qf_ref, kf_ref, vf_ref, sf_ref, qb_ref, kb_ref, vb_ref, sb_ref,
                  hf_ref, hb_ref, C_ref, n_ref, m_ref):
    c = pl.program_id(1)

    @pl.when(c == 0)
    def _():
        C_ref[...] = jnp.zeros_like(C_ref)
        n_ref[...] = jnp.zeros_like(n_ref)
        m_ref[...] = jnp.zeros_like(m_ref)

    row = lax.broadcasted_iota(I32, (ML_CHUNK, ML_CHUNK), 0)
    col = lax.broadcasted_iota(I32, (ML_CHUNK, ML_CHUNK), 1)
    lower = (col <= row)
    upper = (col >= row)
    tri_l = lower.astype(F32)
    tri_u = upper.astype(F32)
    kscale = ML_HEAD_DIM ** -0.5

    for d, (q_ref, k_ref, v_ref, s_ref, o_ref) in enumerate(
            ((qf_ref, kf_ref, vf_ref, sf_ref, hf_ref), (qb_ref, kb_ref, vb_ref, sb_ref, hb_ref))):
        gsub = s_ref[:, SIDE_GATE:SIDE_GATE + LANES]
        gT = gsub.T
        lsig = _log_sigmoid(gsub)
        lsigT = _log_sigmoid(gT)
        if d == 0:
            b_all = _dot(tri_l, lsig, precision=HI)
            b_allT = _dot(lsigT, tri_u, precision=HI)
            mask = lower
        else:
            b_all = _dot(tri_u, lsig, precision=HI)
            b_allT = _dot(lsigT, tri_l, precision=HI)
            mask = upper
        for h in range(ML_HEADS):
            ci = (2 * d) * ML_HEADS + h
            cf = (2 * d + 1) * ML_HEADS + h
            hs = slice(h * ML_HEAD_DIM, (h + 1) * ML_HEAD_DIM)
            q = q_ref[:, hs]
            k = k_ref[:, hs] * kscale
            v = v_ref[:, hs]
            hout = _mlstm_chunk(
                q, k, v,
                gsub[:, ci:ci + 1], lsig[:, cf:cf + 1], b_all[:, cf:cf + 1],
                gT[ci:ci + 1, :], b_allT[cf:cf + 1, :], mask,
                C_ref, n_ref, m_ref, d * ML_HEADS + h)
            o_ref[:, hs] = hout.astype(BF16)


def _mlstm(proj, side, B, L):
    T = B * L
    nc = L // ML_CHUNK
    fwd = lambda cb: (lambda b, c: (b * nc + c, cb))
    bwd = lambda cb: (lambda b, c: (b * nc + nc - 1 - c, cb))
    blk = lambda im: pl.BlockSpec((ML_CHUNK, ML_WIDTH), im)
    sblk = lambda im: pl.BlockSpec((ML_CHUNK, SIDE_W), im)
    nst = 2 * ML_HEADS
    return pl.pallas_call(
        _mlstm_kernel, name="mlstm",
        grid=(B, nc),
        in_specs=[blk(fwd(0)), blk(fwd(1)), blk(fwd(2)), sblk(fwd(0)),
                  blk(bwd(0)), blk(bwd(1)), blk(bwd(2)), sblk(bwd(0))],
        out_specs=[blk(fwd(0)), blk(bwd(0))],
        out_shape=[jax.ShapeDtypeStruct((T, ML_WIDTH), BF16)] * 2,
        scratch_shapes=[pltpu.VMEM((nst, ML_HEAD_DIM, ML_HEAD_DIM), F32),
                        pltpu.VMEM((nst, 1, ML_HEAD_DIM), F32),
                        pltpu.VMEM((nst, 1, 1), F32)],
        compiler_params=_cparams("arbitrary", "arbitrary"),
    )(proj, proj, proj, side, proj, proj, proj, side)


class _FftPlan:
    def __init__(self, L):
        self.L = L
        self.N = 2 * L
        self.N2 = FFT_N2
        self.N1 = self.N // self.N2
        self.N1h = self.N1 // 2
        self.K1 = self.N1h + 1
        self.K1p = -(-self.K1 // SUBLANES) * SUBLANES
        self.PT = self.N2 + FFT_TPAD
        self.P2 = 2 * self.N2 + FFT_KPAD


def _fft_tables(p):
    k1 = jnp.arange(p.K1p, dtype=I32)
    n1 = jnp.arange(p.N1h, dtype=I32)
    n2 = jnp.arange(p.N2, dtype=I32)
    n = p.N2 * n1[None, :] + n2[:, None]
    ph = (k1[None, :, None] * n[:, None, :]) % p.N
    ang = ph.astype(F32) * (2.0 * math.pi / p.N)
    valid = (k1 < p.K1)[None, :, None]
    c = jnp.where(valid, jnp.cos(ang), 0.0)
    s = jnp.where(valid, jnp.sin(ang), 0.0)
    t1 = jnp.concatenate([c, -s], axis=1).astype(BF16)
    wk = jnp.where((k1 == 0) | (k1 == p.N1h), 1.0, 2.0) / p.N
    wk = jnp.where(k1 < p.K1, wk, 0.0)[None, None, :]
    ct = jnp.swapaxes(c, 1, 2) * wk
    st = jnp.swapaxes(s, 1, 2) * wk
    t3 = jnp.concatenate([ct, -st], axis=2).astype(BF16)
    a2 = ((n2[:, None] * n2[None, :]) % p.N2).astype(F32) * (2.0 * math.pi / p.N2)
    c2, s2 = jnp.cos(a2), jnp.sin(a2)
    g2f = jnp.block([[c2, s2], [-s2, c2]]).astype(BF16)
    g2i = jnp.block([[c2, -s2], [s2, c2]]).astype(BF16)
    return t1, t3, g2f, g2i


def _fft_stage1(p, tst_ref, kd_ref, t1_ref):
    def body(n2, carry):
        xs = tst_ref[pl.ds(n2, p.N1h, stride=p.PT), :]
        a = _dot(t1_ref[n2], xs.astype(BF16))
        kd_ref[pl.ds(n2, p.K1p, stride=p.P2), :] = a[:p.K1p]
        kd_ref[pl.ds(p.N2 + n2, p.K1p, stride=p.P2), :] = a[p.K1p:]
        return carry
    lax.fori_loop(0, p.N2, body, 0, unroll=FFT_UNROLL_OUTER)


def _hyfeat_kernel(w1t_ref, w1c_ref, w1s_ref, b1_ref, w2_ref, b2_ref, fr_ref, o_ref, *, L, rows):
    i = pl.program_id(0)
    t = (lax.broadcasted_iota(I32, (rows, 1), 0) + i * rows).astype(F32) / L
    bands = (lax.broadcasted_iota(I32, (1, LANES), 1) + 1).astype(F32)
    ang = (2.0 * math.pi * t) * bands
    z = (t * w1t_ref[...] + _dot(jnp.cos(ang), w1c_ref[...], precision=HI)
         + _dot(jnp.sin(ang), w1s_ref[...], precision=HI) + b1_ref[...])
    h = jnp.sin(fr_ref[0:1, :] * z)
    h = jnp.sin(fr_ref[1:2, :] * (_dot(h, w2_ref[...], precision=HI) + b2_ref[...]))
    o_ref[...] = h


def _hyena_features(L, fw1, fb1, fw2, fb2, freq):
    rows = min(512, L)
    w1c = jnp.zeros((LANES, HY_FFN), F32).at[:HY_BANDS].set(fw1[1:1 + HY_BANDS])
    w1s = jnp.zeros((LANES, HY_FFN), F32).at[:HY_BANDS].set(fw1[1 + HY_BANDS:])
    full = lambda shp: pl.BlockSpec(shp, lambda i: (0,) * len(shp))
    return pl.pallas_call(
        functools.partial(_hyfeat_kernel, L=L, rows=rows), name="hy_feat",
        grid=(L // rows,),
        in_specs=[full((1, HY_FFN)), full((LANES, HY_FFN)), full((LANES, HY_FFN)), full((1, HY_FFN)),
                  full((HY_FFN, HY_FFN)), full((1, HY_FFN)), full((2, HY_FFN))],
        out_specs=pl.BlockSpec((rows, HY_FFN), lambda i: (i, 0)),
        out_shape=jax.ShapeDtypeStruct((L, HY_FFN), F32),
        compiler_params=_cparams("arbitrary"),
    )(fw1[0:1], w1c, w1s, fb1.reshape(1, -1), fw2, fb2.reshape(1, -1), freq)


def _hyfilt_kernel(h2_ref, wf_ref, wb_ref, df_ref, db_ref, skip_ref, t1_ref, g2_ref, o_ref,
                   tsf_ref, tsb_ref, kdf_ref, kdb_ref, *, p):
    L, N2 = p.L, p.N2
    decay_f = jnp.exp(df_ref[...])
    decay_b = jnp.exp(db_ref[...])

    def gen(n1, carry):
        r0 = pl.multiple_of(n1 * N2, N2)
        t = (lax.broadcasted_iota(I32, (N2, 1), 0) + r0).astype(F32) / L
        h2 = h2_ref[pl.ds(r0, N2), :]
        f = _dot(h2, wf_ref[...], precision=HI) * jnp.exp(-t * decay_f)
        b = _dot(h2, wb_ref[...], precision=HI) * jnp.exp(-t * decay_b)
        b = jnp.where(t == 0.0, 0.0, b)
        s0 = pl.multiple_of(n1 * p.PT, SUBLANES)
        tsf_ref[pl.ds(s0, N2), :] = f
        tsb_ref[pl.ds(s0, N2), :] = b
        return carry + jnp.sum(jnp.abs(f) + jnp.abs(b), axis=0, keepdims=True)

    l1 = lax.fori_loop(0, p.N1h, gen, jnp.zeros((1, LANES), F32), unroll=2)
    inv = 1.0 / l1
    _fft_stage1(p, tsf_ref, kdf_ref, t1_ref)
    _fft_stage1(p, tsb_ref, kdb_ref, t1_ref)

    def spec(k1, carry):
        s0 = pl.multiple_of(k1 * p.P2, SUBLANES)
        sf = kdf_ref[pl.ds(s0, 2 * N2), :].astype(BF16)
        sb = kdb_ref[pl.ds(s0, 2 * N2), :].astype(BF16)
        xf = _dot(g2_ref[...], sf)
        xb = _dot(g2_ref[...], sb)
        kr = (xf[:N2] + xb[:N2]) * inv + skip_ref[...]
        ki = (xf[N2:] - xb[N2:]) * inv
        o_ref[k1] = jnp.concatenate([kr, ki], axis=0).astype(BF16)
        return carry
    lax.fori_loop(0, p.K1, spec, 0, unroll=FFT_UNROLL_MID)


def _hyena_spectra(p, h2, fw3, log_decay, skip, t1, g2f):
    nct = HY_WIDTH // LANES
    col = lambda d: (lambda o, c: (0, (2 * o + d) * nct + c))
    full = lambda shp: pl.BlockSpec(shp, lambda o, c: (0,) * len(shp))
    ld = log_decay.reshape(1, -1)
    return pl.pallas_call(
        functools.partial(_hyfilt_kernel, p=p), name="hy_spectra",
        grid=(HY_ORDER, nct),
        in_specs=[full((p.L, HY_FFN)),
                  pl.BlockSpec((HY_FFN, LANES), col(0)), pl.BlockSpec((HY_FFN, LANES), col(1)),
                  pl.BlockSpec((1, LANES), col(0)), pl.BlockSpec((1, LANES), col(1)),
                  pl.BlockSpec((None, 1, LANES), lambda o, c: (o, 0, c)),
                  full(t1.shape), full(g2f.shape)],
        out_specs=pl.BlockSpec((None, p.K1, 2 * p.N2, LANES), lambda o, c: (o, 0, 0, c)),
        out_shape=jax.ShapeDtypeStruct((HY_ORDER, p.K1, 2 * p.N2, HY_WIDTH), BF16),
        scratch_shapes=[pltpu.VMEM((p.N1h * p.PT, LANES), F32), pltpu.VMEM((p.N1h * p.PT, LANES), F32),
                        pltpu.VMEM((p.K1p * p.P2, LANES), F32), pltpu.VMEM((p.K1p * p.P2, LANES), F32)],
        compiler_params=_cparams("arbitrary", "arbitrary"),
    )(h2, fw3, fw3, ld, ld, skip.reshape(HY_ORDER, 1, HY_WIDTH), t1, g2f)


def _conv3_chunk(u_ref, cw_ref, n1, nchunks, L):
    N2 = FFT_N2
    lanes = u_ref.shape[1]
    pk = 2 * SUBLANES
    r0 = pl.multiple_of(n1 * N2, N2)
    cur = u_ref[pl.ds(r0, N2), :].astype(F32)
    pstart = pl.multiple_of(jnp.maximum(r0 - pk, 0), pk)
    nstart = pl.multiple_of(jnp.minimum(r0 + N2, L - pk), pk)
    prev = u_ref[pl.ds(pstart, pk), :].astype(F32)[pk - 1:pk, :]
    nxt = u_ref[pl.ds(nstart, pk), :].astype(F32)[0:1, :]
    prev = prev * jnp.where(n1 > 0, 1.0, 0.0)
    nxt = nxt * jnp.where(n1 < nchunks - 1, 1.0, 0.0)
    row = lax.broadcasted_iota(I32, (N2, lanes), 0)
    up = jnp.where(row == 0, prev, pltpu.roll(cur, 1, axis=0))
    dn = jnp.where(row == N2 - 1, nxt, pltpu.roll(cur, N2 - 1, axis=0))
    return cw_ref[0:1, :] * up + cw_ref[1:2, :] * cur + cw_ref[2:3, :] * dn + cw_ref[3:4, :]


def _pack_lanes(x):
    return _pack_bf16_pair(x[:, :LANES], x[:, LANES:])


def _unpack_lanes(w):
    lo, hi = _unpack_bf16_pair(w)
    return jnp.concatenate([lo.astype(BF16), hi.astype(BF16)], axis=1)


def _hyconv_kernel(ua_ref, ub_ref, cwa_ref, cwb_ref, sp_ref, t1_ref, t3_ref, g2f_ref, g2i_ref,
                   o_ref, tst_ref, kd_ref, *, p, pre_a):
    L, N2 = p.L, p.N2

    def chunk_a(n1):
        if pre_a:
            return _conv3_chunk(ua_ref, cwa_ref, n1, p.N1h, L)
        return ua_ref[pl.ds(pl.multiple_of(n1 * N2, N2), N2), :].astype(F32)

    def load(n1, carry):
        tst_ref[pl.ds(pl.multiple_of(n1 * p.PT, SUBLANES), N2), :] = _pack_lanes(chunk_a(n1))
        return carry
    lax.fori_loop(0, p.N1h, load, 0, unroll=2)

    def stage1(n2, carry):
        xs = _unpack_lanes(tst_ref[pl.ds(n2, p.N1h, stride=p.PT), :])
        w = _pack_lanes(_dot(t1_ref[n2], xs))
        kd_ref[pl.ds(n2, p.K1p, stride=p.P2), :] = w[:p.K1p]
        kd_ref[pl.ds(N2 + n2, p.K1p, stride=p.P2), :] = w[p.K1p:]
        return carry
    lax.fori_loop(0, N2, stage1, 0, unroll=FFT_UNROLL_OUTER)

    def mid(k1, carry):
        s0 = pl.multiple_of(k1 * p.P2, SUBLANES)
        x = _dot(g2f_ref[...], _unpack_lanes(kd_ref[pl.ds(s0, 2 * N2), :]))
        xr, xi = x[:N2], x[N2:]
        kk = sp_ref[k1].astype(F32)
        kr, ki = kk[:N2], kk[N2:]
        y = jnp.concatenate([xr * kr - xi * ki, xr * ki + xi * kr], axis=0).astype(BF16)
        kd_ref[pl.ds(s0, 2 * N2), :] = _pack_lanes(_dot(g2i_ref[...], y))
        return carry
    lax.fori_loop(0, p.K1, mid, 0, unroll=FFT_UNROLL_MID)

    def stage3(n2, carry):
        br = kd_ref[pl.ds(n2, p.K1p, stride=p.P2), :]
        bi = kd_ref[pl.ds(N2 + n2, p.K1p, stride=p.P2), :]
        bc = _unpack_lanes(jnp.concatenate([br, bi], axis=0))
        tst_ref[pl.ds(n2, p.N1h, stride=p.PT), :] = _pack_lanes(_dot(t3_ref[n2], bc))
        return carry
    lax.fori_loop(0, N2, stage3, 0, unroll=FFT_UNROLL_OUTER)

    def store(n1, carry):
        lo, hi = _unpack_bf16_pair(tst_ref[pl.ds(pl.multiple_of(n1 * p.PT, SUBLANES), N2), :])
        y = jnp.concatenate([lo, hi], axis=1)
        g = _conv3_chunk(ub_ref, cwb_ref, n1, p.N1h, L)
        o_ref[pl.ds(pl.multiple_of(n1 * N2, N2), N2), :] = (g * y).astype(BF16)
        return carry
    lax.fori_loop(0, p.N1h, store, 0, unroll=2)


def _hyconv(p, a_arr, a_col, b_arr, b_col, cw_pack, spectra, order, tables, B, pre_a):
    L = p.L
    T = B * L
    nct = HY_WIDTH // HY_CT
    t1, t3, g2f, g2i = tables
    seq_bytes = L * HY_CT * 2
    const_bytes = 2 * (t1.size + t3.size + p.K1 * 2 * p.N2 * HY_CT)
    scratch_bytes = 4 * LANES * (p.N1h * p.PT + p.K1p * p.P2)
    single = 6 * seq_bytes + 2 * const_bytes + scratch_bytes > VMEM_LIMIT
    mode = dict(pipeline_mode=pl.Buffered(1)) if single else {}
    full = lambda shp: pl.BlockSpec(shp, lambda c, b: (0,) * len(shp), **mode)
    acb, bcb = a_col // HY_CT, b_col // HY_CT
    cwa_cb = (a_col - COL_HY) // HY_CT if pre_a else 0
    cwb_cb = (b_col - COL_HY) // HY_CT
    return pl.pallas_call(
        functools.partial(_hyconv_kernel, p=p, pre_a=pre_a), name="hy_conv",
        grid=(nct, B),
        in_specs=[pl.BlockSpec((L, HY_CT), lambda c, b: (b, acb + c), **mode),
                  pl.BlockSpec((L, HY_CT), lambda c, b: (b, bcb + c), **mode),
                  pl.BlockSpec((SUBLANES, HY_CT), lambda c, b: (0, cwa_cb + c)),
                  pl.BlockSpec((SUBLANES, HY_CT), lambda c, b: (0, cwb_cb + c)),
                  pl.BlockSpec((None, p.K1, 2 * p.N2, HY_CT), lambda c, b: (order, 0, 0, c), **mode),
                  full(t1.shape), full(t3.shape), full(g2f.shape), full(g2i.shape)],
        out_specs=pl.BlockSpec((L, HY_CT), lambda c, b: (b, c)),
        out_shape=jax.ShapeDtypeStruct((T, HY_WIDTH), BF16),
        scratch_shapes=[pltpu.VMEM((p.N1h * p.PT, LANES), U32), pltpu.VMEM((p.K1p * p.P2, LANES), U32)],
        compiler_params=_cparams("arbitrary", "arbitrary"),
    )(a_arr, b_arr, cw_pack, cw_pack, spectra, t1, t3, g2f, g2i)


def _mlaprep_kernel(cq_ref, ckv_ref, side_ref, cs_ref, gq_ref, gkv_ref, wuq_ref, wuk_ref, wuvt_ref,
                    q_ref, k_ref, vt_ref):
    cqn = _rms(cq_ref[...].astype(F32), gq_ref[...]).astype(BF16)
    ckvn = _rms(ckv_ref[...].astype(F32), gkv_ref[...]).astype(BF16)
    qa = _dot(cqn, wuq_ref[...])
    kn = _dot(ckvn, wuk_ref[...])
    vt = _dot_nt(wuvt_ref[...], ckvn)
    cs = cs_ref[...]
    cos2, sin2 = cs[:, :MLA_ROPE], cs[:, MLA_ROPE:]
    side = side_ref[...]
    k_rope = (side[:, :MLA_ROPE] * cos2 + side[:, MLA_ROPE:2 * MLA_ROPE] * sin2).astype(BF16)
    scale = MLA_QK ** -0.5 * math.log2(math.e)
    hw = MLA_NOPE + 2 * MLA_ROPE
    for h in range(MLA_HEADS):
        b0 = h * hw
        q_rope = qa[:, b0 + MLA_NOPE:b0 + MLA_QK] * cos2 + qa[:, b0 + MLA_QK:b0 + hw] * sin2
        q_ref[h, :, :MLA_NOPE] = (qa[:, b0:b0 + MLA_NOPE] * scale).astype(BF16)
        q_ref[h, :, MLA_NOPE:] = (q_rope * scale).astype(BF16)
        k_ref[h, :, :MLA_NOPE] = kn[:, h * MLA_NOPE:(h + 1) * MLA_NOPE].astype(BF16)
        k_ref[h, :, MLA_NOPE:] = k_rope
        vt_ref[h] = vt[h * MLA_V:(h + 1) * MLA_V, :].astype(BF16)


def _mla_prep(proj, side, cs, gq, gkv, wuq_ext, wuk, wuvt, B, L):
    T = B * L
    tm = min(512, L)
    nb = L // tm
    hw = MLA_NOPE + 2 * MLA_ROPE
    full = lambda shp: pl.BlockSpec(shp, lambda i: (0,) * len(shp))
    oidx = lambda i: (i // nb, 0, i % nb, 0)
    return pl.pallas_call(
        _mlaprep_kernel, name="mla_prep",
        grid=(T // tm,),
        in_specs=[pl.BlockSpec((tm, MLA_Q_RANK), lambda i: (i, COL_CQ // MLA_Q_RANK)),
                  pl.BlockSpec((tm, MLA_KV_RANK), lambda i: (i, COL_CKV // MLA_KV_RANK)),
                  pl.BlockSpec((tm, SIDE_W), lambda i: (i, 0)),
                  pl.BlockSpec((tm, 2 * MLA_ROPE), lambda i: (i % nb, 0)),
                  full((1, MLA_Q_RANK)), full((1, MLA_KV_RANK)),
                  full((MLA_Q_RANK, MLA_HEADS * hw)), full((MLA_KV_RANK, MLA_HEADS * MLA_NOPE)),
                  full((MLA_HEADS * MLA_V, MLA_KV_RANK))],
        out_specs=[pl.BlockSpec((None, MLA_HEADS, tm, MLA_QK), oidx),
                   pl.BlockSpec((None, MLA_HEADS, tm, MLA_QK), oidx),
                   pl.BlockSpec((None, MLA_HEADS, MLA_V, tm), lambda i: (i // nb, 0, 0, i % nb))],
        out_shape=[jax.ShapeDtypeStruct((B, MLA_HEADS, L, MLA_QK), BF16),
                   jax.ShapeDtypeStruct((B, MLA_HEADS, L, MLA_QK), BF16),
                   jax.ShapeDtypeStruct((B, MLA_HEADS, MLA_V, L), BF16)],
        compiler_params=_cparams("arbitrary"),
    )(proj, proj, side, cs, gq, gkv, wuq_ext, wuk, wuvt)


def _flash_kernel(q_ref, k_ref, vt_ref, o_ref, *, tq, tk, nsplit):
    L = k_ref.shape[0]
    sub = tq // nsplit
    qs = [q_ref[pl.ds(i * sub, sub), :] for i in range(nsplit)]

    def step(j, carry):
        k0 = pl.multiple_of(j * tk, tk)
        kj = k_ref[pl.ds(k0, tk), :]
        vtj = vt_ref[:, pl.ds(k0, tk)]
        out = []
        for i in range(nsplit):
            m_prev, l_prev, acc = carry[i]
            st = _dot_nt(kj, qs[i])
            m_new = jnp.maximum(m_prev, jnp.max(st, axis=0, keepdims=True))
            alpha = jnp.exp2(m_prev - m_new)
            pt = jnp.exp2(st - m_new)
            l_new = alpha * l_prev + jnp.sum(pt, axis=0, keepdims=True)
            acc = alpha * acc + _dot(vtj, pt.astype(BF16))
            out.append((m_new, l_new, acc))
        return tuple(out)

    init = tuple((jnp.full((1, sub), -jnp.inf, F32), jnp.zeros((1, sub), F32), jnp.zeros((MLA_V, sub), F32))
                 for _ in range(nsplit))
    res = lax.fori_loop(0, L // tk, step, init)
    for i in range(nsplit):
        _, l_fin, acc = res[i]
        o_ref[pl.ds(i * sub, sub), :] = (acc / l_fin).T.astype(BF16)


def _flash(q, k, v, B, L):
    tq = min(ATTN_TQ, L)
    tk = min(ATTN_TK, L)
    nq = L // tq
    return pl.pallas_call(
        functools.partial(_flash_kernel, tq=tq, tk=tk, nsplit=ATTN_SPLIT), name="mla_attn",
        grid=(B, MLA_HEADS, nq),
        in_specs=[pl.BlockSpec((None, None, tq, MLA_QK), lambda b, h, i: (b, h, i, 0)),
                  pl.BlockSpec((None, None, L, MLA_QK), lambda b, h, i: (b, h, 0, 0)),
                  pl.BlockSpec((None, None, MLA_V, L), lambda b, h, i: (b, h, 0, 0))],
        out_specs=pl.BlockSpec((tq, MLA_V), lambda b, h, i: (b * nq + i, h)),
        out_shape=jax.ShapeDtypeStruct((B * L, MLA_WIDTH), BF16),
        compiler_params=_cparams("arbitrary", "arbitrary", "arbitrary"),
    )(q, k, v)


def _merge_kernel(hf_ref, hb_ref, mlo_ref, mlg_ref, ohy_ref, omla_ref, gml_ref, ghy_ref, gmla_ref,
                  wml_ref, why_ref, wmla_ref, o_ref, oml_ref):
    @pl.when(pl.program_id(1) == 0)
    def _():
        for h in range(ML_HEADS):
            hs = slice(h * ML_HEAD_DIM, (h + 1) * ML_HEAD_DIM)
            x = hf_ref[:, hs].astype(F32) + hb_ref[:, hs].astype(F32)
            y = _rms(x, mlg_ref[:, hs])
            oml_ref[:, hs] = (y * _sigmoid(mlo_ref[:, hs].astype(F32))).astype(BF16)

    acc = _sigmoid(gml_ref[...].astype(F32)) * _dot(oml_ref[...], wml_ref[...])
    acc += _sigmoid(ghy_ref[...].astype(F32)) * _dot(ohy_ref[...], why_ref[...])
    acc += _sigmoid(gmla_ref[...].astype(F32)) * _dot(omla_ref[...], wmla_ref[...])
    o_ref[...] = acc.astype(BF16)


def _merge(hf, hb, proj, ml_norm_g, o_hy, o_mla, wml, why, wmla, L):
    T = hf.shape[0]
    tm = min(1024, L)
    tn = 512
    row = lambda cb: (lambda i, j: (i, cb))
    gate = lambda k: (lambda i, j: (i, (COL_BRG + k * D_MODEL) // tn + j))
    wsp = pl.BlockSpec((ML_WIDTH, tn), lambda i, j: (0, j))
    return pl.pallas_call(
        _merge_kernel, name="merge",
        grid=(T // tm, D_MODEL // tn),
        in_specs=[pl.BlockSpec((tm, ML_WIDTH), row(0)), pl.BlockSpec((tm, ML_WIDTH), row(0)),
                  pl.BlockSpec((tm, ML_WIDTH), row(3)),
                  pl.BlockSpec((1, ML_WIDTH), lambda i, j: (0, 0)),
                  pl.BlockSpec((tm, HY_WIDTH), row(0)), pl.BlockSpec((tm, MLA_WIDTH), row(0)),
                  pl.BlockSpec((tm, tn), gate(0)), pl.BlockSpec((tm, tn), gate(1)), pl.BlockSpec((tm, tn), gate(2)),
                  wsp, wsp, wsp],
        out_specs=pl.BlockSpec((tm, tn), lambda i, j: (i, j)),
        out_shape=jax.ShapeDtypeStruct((T, D_MODEL), BF16),
        scratch_shapes=[pltpu.VMEM((tm, ML_WIDTH), BF16)],
        compiler_params=_cparams("arbitrary", "arbitrary"),
    )(hf, hb, proj, ml_norm_g, o_hy, o_mla, proj, proj, proj, wml, why, wmla)


def _outproj_kernel(m_ref, w_ref, x_ref, g_ref, o_ref):
    o_ref[...] = x_ref[...] + g_ref[0] * _dot(m_ref[...], w_ref[...])


def _out_proj(merged, w_out, x2d, g1, L):
    T = x2d.shape[0]
    tm = min(1024, L)
    tn = 512
    return pl.pallas_call(
        _outproj_kernel, name="out_proj",
        grid=(T // tm, D_MODEL // tn),
        in_specs=[pl.BlockSpec((tm, D_MODEL), lambda i, j: (i, 0)),
                  pl.BlockSpec((D_MODEL, tn), lambda i, j: (0, j)),
                  pl.BlockSpec((tm, tn), lambda i, j: (i, j)),
                  pl.BlockSpec((1, 1, tn), lambda i, j: ((i * tm) // L, 0, j))],
        out_specs=pl.BlockSpec((tm, tn), lambda i, j: (i, j)),
        out_shape=jax.ShapeDtypeStruct((T, D_MODEL), F32),
        compiler_params=_cparams("arbitrary", "arbitrary"),
    )(merged, w_out, x2d, g1)


def _pack_bf16_pair(lo, hi):
    lb = lax.bitcast_convert_type(lo.astype(BF16).astype(F32), U32)
    hb = lax.bitcast_convert_type(hi.astype(BF16).astype(F32), U32)
    return hb | (lb >> 16)


def _unpack_bf16_pair(w):
    lo = lax.bitcast_convert_type(w << 16, F32)
    hi = lax.bitcast_convert_type(w & jnp.uint32(0xFFFF0000), F32)
    return lo, hi


def _route_kernel(x_ref, g_ref, sc_ref, sh_ref, wr_ref, rb_ref, hp_ref, idx_ref, wgt_ref):
    half = D_MODEL // 2
    hn = _rms(x_ref[...], g_ref[...]) * (1.0 + sc_ref[0]) + sh_ref[0]
    hp_ref[...] = _pack_bf16_pair(hn[:, :half], hn[:, half:])
    logits = _dot_nt(wr_ref[...], hn, precision=HI)
    scores = _sigmoid(logits)
    sel = scores + rb_ref[...]
    rows = [sel[e:e + 1, :] for e in range(N_EXPERTS)]
    srow = [scores[e:e + 1, :] for e in range(N_EXPERTS)]
    gs = []
    for g in range(N_GROUPS):
        r = rows[g * EXPERTS_PER_GROUP:(g + 1) * EXPERTS_PER_GROUP]
        best = None
        for a in range(EXPERTS_PER_GROUP):
            for b in range(a + 1, EXPERTS_PER_GROUP):
                s = r[a] + r[b]
                best = s if best is None else jnp.maximum(best, s)
        gs.append(best)
    gsel = jnp.zeros_like(gs[0], dtype=I32)
    gbest = gs[0]
    for g in range(1, N_GROUPS):
        better = gs[g] > gbest
        gsel = jnp.where(better, g, gsel)
        gbest = jnp.where(better, gs[g], gbest)
    masked = [jnp.where(gsel == (e // EXPERTS_PER_GROUP), rows[e], -jnp.inf) for e in range(N_EXPERTS)]
    picks = []
    for _ in range(2):
        bi = jnp.zeros_like(gsel)
        bv = masked[0]
        bs = srow[0]
        for e in range(1, N_EXPERTS):
            better = masked[e] > bv
            bi = jnp.where(better, e, bi)
            bv = jnp.where(better, masked[e], bv)
            bs = jnp.where(better, srow[e], bs)
        picks.append((bi, bs))
        masked = [jnp.where(bi == e, -jnp.inf, masked[e]) for e in range(N_EXPERTS)]
    (i1, s1), (i2, s2) = picks
    tot = s1 + s2
    idx_ref[...] = jnp.concatenate([i1, i2], axis=0)
    wgt_ref[...] = jnp.concatenate([s1 / tot, s2 / tot], axis=0)


def _route(x2d, g, sc, sh, w_router, router_bias, L):
    T = x2d.shape[0]
    tm = min(512, L)
    bidx = lambda i: ((i * tm) // L, 0, 0)
    return pl.pallas_call(
        _route_kernel, name="route",
        grid=(T // tm,),
        in_specs=[pl.BlockSpec((tm, D_MODEL), lambda i: (i, 0)),
                  pl.BlockSpec((1, D_MODEL), lambda i: (0, 0)),
                  pl.BlockSpec((1, 1, D_MODEL), bidx), pl.BlockSpec((1, 1, D_MODEL), bidx),
                  pl.BlockSpec((N_EXPERTS, D_MODEL), lambda i: (0, 0)),
                  pl.BlockSpec((N_EXPERTS, 1), lambda i: (0, 0))],
        out_specs=[pl.BlockSpec((tm, D_MODEL // 2), lambda i: (i, 0)),
                   pl.BlockSpec((2, tm), lambda i: (0, i)),
                   pl.BlockSpec((2, tm), lambda i: (0, i))],
        out_shape=[jax.ShapeDtypeStruct((T, D_MODEL // 2), U32),
                   jax.ShapeDtypeStruct((2, T), I32),
                   jax.ShapeDtypeStruct((2, T), F32)],
        compiler_params=_cparams("arbitrary"),
    )(x2d, g, sc, sh, w_router.T, router_bias.reshape(N_EXPERTS, 1))


def _moe_plan(idx, T):
    A = 2 * T
    e = idx.reshape(A)
    tok = jnp.tile(jnp.arange(T, dtype=I32), 2)
    onehot = (e[:, None] == jnp.arange(N_EXPERTS, dtype=I32)[None, :]).astype(I32)
    csum = jnp.cumsum(onehot, axis=0)
    counts = csum[-1]
    rank = jnp.sum(csum * onehot, axis=1) - 1
    padded = ((counts + MOE_TILE - 1) // MOE_TILE) * MOE_TILE
    ends = jnp.cumsum(padded)
    starts = ends - padded
    pos = jnp.sum(starts[None, :] * onehot, axis=1) + rank
    R = A + N_EXPERTS * MOE_TILE
    src = jnp.zeros((R,), I32).at[pos].set(tok, unique_indices=True)
    n_tiles = R // MOE_TILE
    tile_start = jnp.arange(n_tiles, dtype=I32) * MOE_TILE
    tile_e = jnp.minimum(jnp.sum((tile_start[:, None] >= ends[None, :]).astype(I32), axis=1), N_EXPERTS - 1)
    num_used = (ends[-1] // MOE_TILE).astype(I32).reshape(1)
    return src, pos.reshape(2, T).astype(I32), tile_e.astype(I32), num_used


def _row_gather_copy(src_hbm, dst_ref, sem, src_row, dst_row):
    return pltpu.make_async_copy(src_hbm.at[pl.ds(src_row, 1)], dst_ref.at[pl.ds(dst_row, 1)], sem)


def _issue_row_gathers(src_hbm, idx_ref, dst_ref, sem, rows, dst_base=0):
    for r in range(rows):
        _row_gather_copy(src_hbm, dst_ref, sem, idx_ref[0, 0, r], dst_base + r).start()


def _wait_row_gathers(src_hbm, dst_ref, sem):
    pltpu.make_async_copy(src_hbm.at[pl.ds(0, dst_ref.shape[0])], dst_ref, sem).wait()


def _expert_kernel(te_ref, nt_ref, idx_ref, idxn_ref, hp_hbm, wgu_ref, wd_ref, o_ref, xbuf_ref, sems):
    t = pl.program_id(0)
    nt = nt_ref[0]
    slot = lax.rem(t, 2)
    half = D_MODEL // 2

    @pl.when(t == 0)
    def _():
        _issue_row_gathers(hp_hbm, idx_ref, xbuf_ref.at[0], sems.at[0], MOE_TILE)

    @pl.when(t < nt)
    def _():
        _wait_row_gathers(hp_hbm, xbuf_ref.at[slot], sems.at[slot])
        _issue_row_gathers(hp_hbm, idxn_ref, xbuf_ref.at[1 - slot], sems.at[1 - slot], MOE_TILE)
        lo, hi = _unpack_bf16_pair(xbuf_ref[slot])
        h1 = _dot(lo.astype(BF16), wgu_ref[:half, :]) + _dot(hi.astype(BF16), wgu_ref[half:, :])
        a, b = h1[:, :D_EXPERT], h1[:, D_EXPERT:]
        act = (a * _sigmoid(a) * b).astype(BF16)
        y = _dot(act, wd_ref[...])
        o_ref[...] = _pack_bf16_pair(y[:, :half], y[:, half:])

    @pl.when(t == nt - 1)
    def _():
        _wait_row_gathers(hp_hbm, xbuf_ref.at[1 - slot], sems.at[1 - slot])

    @pl.when(t >= nt)
    def _():
        o_ref[...] = jnp.zeros_like(o_ref)


def _experts(hp, src, tile_e, num_used, wgu, wd, l):
    R = src.shape[0]
    nt = R // MOE_TILE
    half = D_MODEL // 2
    src3 = src.reshape(nt, 1, MOE_TILE)
    grid_spec = pltpu.PrefetchScalarGridSpec(
        num_scalar_prefetch=2,
        grid=(nt,),
        in_specs=[pl.BlockSpec((1, 1, MOE_TILE), lambda t, te, n: (t, 0, 0), memory_space=pltpu.SMEM),
                  pl.BlockSpec((1, 1, MOE_TILE), lambda t, te, n: (jnp.minimum(t + 1, n[0] - 1), 0, 0),
                               memory_space=pltpu.SMEM),
                  pl.BlockSpec(memory_space=pl.ANY),
                  pl.BlockSpec((None, None, D_MODEL, 2 * D_EXPERT), lambda t, te, n: (l, te[t], 0, 0)),
                  pl.BlockSpec((None, None, D_EXPERT, D_MODEL), lambda t, te, n: (l, te[t], 0, 0))],
        out_specs=pl.BlockSpec((MOE_TILE, half), lambda t, te, n: (t, 0)),
        scratch_shapes=[pltpu.VMEM((2, MOE_TILE, half), U32), pltpu.SemaphoreType.DMA((2,))],
    )
    return pl.pallas_call(
        _expert_kernel, name="moe_experts",
        grid_spec=grid_spec,
        out_shape=jax.ShapeDtypeStruct((R, half), U32),
        compiler_params=_cparams("arbitrary"),
    )(tile_e, num_used, src3, src3, hp, wgu, wd)


def _combine_kernel(p1_ref, p2_ref, p1n_ref, p2n_ref, ys_hbm, w_ref, x_ref, g_ref, fg_ref, o_ref, buf_ref, sems,
                    *, rows, final):
    i = pl.program_id(0)
    n = pl.num_programs(0)
    slot = lax.rem(i, 2)
    half = D_MODEL // 2

    @pl.when(i == 0)
    def _():
        _issue_row_gathers(ys_hbm, p1_ref, buf_ref.at[0], sems.at[0], rows)
        _issue_row_gathers(ys_hbm, p2_ref, buf_ref.at[0], sems.at[0], rows, rows)

    _wait_row_gathers(ys_hbm, buf_ref.at[slot], sems.at[slot])
    _issue_row_gathers(ys_hbm, p1n_ref, buf_ref.at[1 - slot], sems.at[1 - slot], rows)
    _issue_row_gathers(ys_hbm, p2n_ref, buf_ref.at[1 - slot], sems.at[1 - slot], rows, rows)

    w = w_ref[...]
    w1, w2 = w[:, 0:1], w[:, 1:2]
    lo1, hi1 = _unpack_bf16_pair(buf_ref[slot, pl.ds(0, rows), :])
    lo2, hi2 = _unpack_bf16_pair(buf_ref[slot, pl.ds(rows, rows), :])
    g = g_ref[0]
    xlo = x_ref[:, :half] + g[:, :half] * (w1 * lo1 + w2 * lo2)
    xhi = x_ref[:, half:] + g[:, half:] * (w1 * hi1 + w2 * hi2)
    if final:
        ms = (jnp.sum(xlo * xlo, axis=-1, keepdims=True) + jnp.sum(xhi * xhi, axis=-1, keepdims=True)) / D_MODEL
        r = lax.rsqrt(ms + EPS)
        xlo = xlo * r * fg_ref[:, :half]
        xhi = xhi * r * fg_ref[:, half:]
    o_ref[:, :half] = xlo
    o_ref[:, half:] = xhi

    @pl.when(i == n - 1)
    def _():
        _wait_row_gathers(ys_hbm, buf_ref.at[1 - slot], sems.at[1 - slot])


def _combine(ys, pos, wgt, x2d, g2, final_g, L, final):
    T = x2d.shape[0]
    rows = GATHER_TILE
    nt = T // rows
    half = D_MODEL // 2
    cur = lambda: pl.BlockSpec((1, 1, rows), lambda i: (i, 0, 0), memory_space=pltpu.SMEM)
    nxt = lambda: pl.BlockSpec((1, 1, rows), lambda i: (jnp.minimum(i + 1, nt - 1), 0, 0), memory_space=pltpu.SMEM)
    p1 = pos[0].reshape(nt, 1, rows)
    p2 = pos[1].reshape(nt, 1, rows)
    return pl.pallas_call(
        functools.partial(_combine_kernel, rows=rows, final=final), name="moe_combine",
        grid=(nt,),
        in_specs=[cur(), cur(), nxt(), nxt(),
                  pl.BlockSpec(memory_space=pl.ANY),
                  pl.BlockSpec((rows, 2), lambda i: (i, 0)),
                  pl.BlockSpec((rows, D_MODEL), lambda i: (i, 0)),
                  pl.BlockSpec((1, 1, D_MODEL), lambda i: ((i * rows) // L, 0, 0)),
                  pl.BlockSpec((1, D_MODEL), lambda i: (0, 0))],
        out_specs=pl.BlockSpec((rows, D_MODEL), lambda i: (i, 0)),
        out_shape=jax.ShapeDtypeStruct((T, D_MODEL), F32),
        scratch_shapes=[pltpu.VMEM((2, 2 * rows, half), U32), pltpu.SemaphoreType.DMA((2,))],
        compiler_params=_cparams("arbitrary"),
    )(p1, p2, p1, p2, ys, wgt.T, x2d, g2, final_g)


def _swap_halves(w):
    h = w.shape[-1] // 2
    return jnp.concatenate([w[..., h:], w[..., :h]], axis=-1)


def _prep_layer(P, l):
    w = P["w_in"][l]
    o_g = 4 * ML_WIDTH
    o_hy = o_g + 4 * ML_HEADS
    o_cq = o_hy + 3 * HY_WIDTH
    o_ckv = o_cq + MLA_Q_RANK
    o_kr = o_ckv + MLA_KV_RANK
    o_br = o_kr + MLA_ROPE
    kr = w[:, o_kr:o_br]
    side_pad = jnp.zeros((D_MODEL, SIDE_W - SIDE_GATE - 4 * ML_HEADS), F32)
    w_in_r = jnp.concatenate(
        [w[:, :o_g], w[:, o_hy:o_cq], w[:, o_br:], w[:, o_cq:o_ckv], w[:, o_ckv:o_kr],
         kr, _swap_halves(kr), w[:, o_g:o_hy], side_pad], axis=1).astype(BF16)
    side_b = jnp.zeros((1, SIDE_W), F32).at[0, SIDE_GATE:SIDE_GATE + 4 * ML_HEADS].set(P["ml_gate_b"][l].reshape(-1))
    wuq = P["w_uq"][l].reshape(MLA_Q_RANK, MLA_HEADS, MLA_QK)
    qr = wuq[..., MLA_NOPE:]
    wuq_ext = jnp.concatenate([wuq, _swap_halves(qr)], axis=-1).reshape(MLA_Q_RANK, -1).astype(BF16)
    wukv = P["w_ukv"][l].reshape(MLA_KV_RANK, MLA_HEADS, MLA_NOPE + MLA_V)
    cw = jnp.concatenate([P["hy_conv_w"][l], P["hy_conv_b"][l][None, :],
                          jnp.zeros((SUBLANES - 4, 3 * HY_WIDTH), F32)], axis=0)
    return dict(
        w_in_r=w_in_r, side_b=side_b, wuq_ext=wuq_ext, cw=cw,
        wuk=wukv[..., :MLA_NOPE].reshape(MLA_KV_RANK, -1).astype(BF16),
        wuvt=wukv[..., MLA_NOPE:].reshape(MLA_KV_RANK, -1).T.astype(BF16),
        wml=P["w_br_ml"][l].astype(BF16), why=P["w_br_hy"][l].astype(BF16), wmla=P["w_br_mla"][l].astype(BF16),
        w_out=P["w_out"][l].astype(BF16),
        norm1_g=P["norm1_g"][l].reshape(1, -1), norm2_g=P["norm2_g"][l].reshape(1, -1),
        ml_norm_g=P["ml_norm_g"][l].reshape(1, -1),
        gq=P["mla_q_norm_g"][l].reshape(1, -1), gkv=P["mla_kv_norm_g"][l].reshape(1, -1),
    )


def _rope_table(L):
    pos = jnp.arange(L, dtype=F32)
    inv = ROPE_BASE ** (-jnp.arange(0, MLA_ROPE, 2, dtype=F32) / MLA_ROPE)
    ang = pos[:, None] * inv[None, :]
    cos, sin = jnp.cos(ang), jnp.sin(ang)
    return jnp.concatenate([cos, cos, -sin, sin], axis=-1)


def _encoder(x, mods, P, W, wgu_bf, wd_bf):
    B, L, _ = x.shape
    T = B * L
    x2d = x.reshape(T, D_MODEL)
    plan = _FftPlan(L)
    tables = _fft_tables(plan)
    cs = _rope_table(L)
    depth = len(W)
    for l in range(depth):
        wl = W[l]
        sh1, sc1, g1, sh2, sc2, g2 = [mods[l][:, k].reshape(B, 1, D_MODEL) for k in range(6)]
        proj, side = _in_proj(x2d, wl["norm1_g"], sc1, sh1, wl["w_in_r"], wl["side_b"], L)
        hf, hb = _mlstm(proj, side, B, L)
        h2 = _hyena_features(L, P["hy_fw1"][l], P["hy_fb1"][l], P["hy_fw2"][l], P["hy_fb2"][l], P["hy_freq"][l])
        spectra = _hyena_spectra(plan, h2, P["hy_fw3"][l], P["hy_log_decay"][l], P["hy_skip"][l], tables[0], tables[2])
        z = _hyconv(plan, proj, COL_HY, proj, COL_HY + HY_WIDTH, wl["cw"], spectra, 0, tables, B, True)
        o_hy = _hyconv(plan, z, 0, proj, COL_HY + 2 * HY_WIDTH, wl["cw"], spectra, 1, tables, B, False)
        q, k, v = _mla_prep(proj, side, cs, wl["gq"], wl["gkv"], wl["wuq_ext"], wl["wuk"], wl["wuvt"], B, L)
        o_mla = _flash(q, k, v, B, L)
        merged = _merge(hf, hb, proj, wl["ml_norm_g"], o_hy, o_mla, wl["wml"], wl["why"], wl["wmla"], L)
        x2d = _out_proj(merged, wl["w_out"], x2d, g1, L)
        hp, idx, wgt = _route(x2d, wl["norm2_g"], sc2, sh2, P["w_router"], P["router_bias"], L)
        src, pos, tile_e, num_used = _moe_plan(idx, T)
        ys = _experts(hp, src, tile_e, num_used, wgu_bf, wd_bf, l)
        x2d = _combine(ys, pos, wgt, x2d, g2, P["final_g"].reshape(1, -1), L, final=(l == depth - 1))
    return x2d.reshape(B, L, D_MODEL)


def kernel(x_prompt, x_sample, c_prompt, c_sample, w_ada, b_ada, norm1_g, norm2_g, w_in, ml_gate_b, ml_norm_g,
           hy_conv_w, hy_conv_b, hy_fw1, hy_fb1, hy_fw2, hy_fb2, hy_fw3, hy_freq, hy_log_decay, hy_skip,
           mla_q_norm_g, mla_kv_norm_g, w_uq, w_ukv, w_br_ml, w_br_hy, w_br_mla, w_out, w_router, router_bias,
           w_gate_up, w_down, final_g):
    P = dict(w_ada=w_ada, b_ada=b_ada, norm1_g=norm1_g, norm2_g=norm2_g, w_in=w_in, ml_gate_b=ml_gate_b,
             ml_norm_g=ml_norm_g, hy_conv_w=hy_conv_w, hy_conv_b=hy_conv_b, hy_fw1=hy_fw1, hy_fb1=hy_fb1,
             hy_fw2=hy_fw2, hy_fb2=hy_fb2, hy_fw3=hy_fw3, hy_freq=hy_freq, hy_log_decay=hy_log_decay,
             hy_skip=hy_skip, mla_q_norm_g=mla_q_norm_g, mla_kv_norm_g=mla_kv_norm_g, w_uq=w_uq, w_ukv=w_ukv,
             w_br_ml=w_br_ml, w_br_hy=w_br_hy, w_br_mla=w_br_mla, w_out=w_out, w_router=w_router,
             router_bias=router_bias, w_gate_up=w_gate_up, w_down=w_down, final_g=final_g)
    depth = w_in.shape[0]
    W = [_prep_layer(P, l) for l in range(depth)]
    wgu_bf = w_gate_up.astype(BF16)
    wd_bf = w_down.astype(BF16)
    bp, bs = c_prompt.shape[0], c_sample.shape[0]
    rows = -(-(bp + bs) // SUBLANES) * SUBLANES
    c_all = jnp.zeros((rows, D_MODEL), F32).at[:bp].set(c_prompt).at[bp:bp + bs].set(c_sample)
    mods_p, mods_s = [], []
    for l in range(depth):
        mod = _ada(c_all, w_ada, b_ada, l).reshape(rows, 6, D_MODEL)
        mods_p.append(mod[:bp])
        mods_s.append(mod[bp:bp + bs])
    y_prompt = _encoder(x_prompt, mods_p, P, W, wgu_bf, wd_bf)
    y_sample = _encoder(x_sample, mods_s, P, W, wgu_bf, wd_bf)
    return (y_prompt, y_sample)
```

```python
import functools
import math

import jax
import jax.numpy as jnp
from jax import lax
from jax.experimental import pallas as pl
from jax.experimental.pallas import tpu as pltpu

F32 = jnp.float32
BF16 = jnp.bfloat16
U32 = jnp.uint32
I32 = jnp.int32

D_MODEL = 2048
ML_HEADS = 4
ML_HEAD_DIM = 256
ML_WIDTH = ML_HEADS * ML_HEAD_DIM
ML_CHUNK = 128
HY_WIDTH = 1024
HY_ORDER = 2
HY_BANDS = 16
HY_FFN = 64
MLA_HEADS = 8
MLA_Q_RANK = 512
MLA_KV_RANK = 256
MLA_NOPE = 128
MLA_ROPE = 64
MLA_V = 128
MLA_WIDTH = MLA_HEADS * MLA_V
MLA_QK = MLA_NOPE + MLA_ROPE
ROPE_BASE = 10000.0
N_EXPERTS = 16
N_GROUPS = 4
EXPERTS_PER_GROUP = N_EXPERTS // N_GROUPS
D_EXPERT = 1024
EPS = 1e-6

LANES = 128
SUBLANES = 8
VMEM_LIMIT = 56 * 1024 * 1024

COL_ML = 0
COL_HY = 4 * ML_WIDTH
COL_BRG = COL_HY + 3 * HY_WIDTH
COL_CQ = COL_BRG + 3 * D_MODEL
COL_CKV = COL_CQ + MLA_Q_RANK
COL_SIDE = COL_CKV + MLA_KV_RANK
SIDE_W = 256
SIDE_GATE = 128
IN_COLS_R = COL_SIDE + SIDE_W
IN_TN = 1024

FFT_N2 = 128
FFT_TPAD = SUBLANES
FFT_KPAD = SUBLANES
FFT_UNROLL_OUTER = 16
FFT_UNROLL_MID = 8
HY_CT = 2 * LANES

MOE_TILE = 256
GATHER_TILE = 256
ROUTE_TM = 1024
ATTN_TQ = 2048
ATTN_TK = 2048
ATTN_SPLIT = 2


def _cparams(*sem):
    return pltpu.CompilerParams(dimension_semantics=sem, vmem_limit_bytes=VMEM_LIMIT)


def _rms(x, g):
    return x * lax.rsqrt(jnp.mean(x * x, axis=-1, keepdims=True) + EPS) * g


def _log_sigmoid(x):
    return -(jnp.maximum(-x, 0.0) + jnp.log1p(jnp.exp(-jnp.abs(x))))


def _sigmoid(x):
    return 1.0 / (1.0 + jnp.exp(-x))


def _dot(a, b, **kw):
    return jnp.dot(a, b, preferred_element_type=F32, **kw)


def _dot_nt(a, b, **kw):
    return lax.dot_general(a, b, (((1,), (1,)), ((), ())), preferred_element_type=F32, **kw)


HI = lax.Precision.HIGHEST


def _ada_kernel(c_ref, w_ref, b_ref, o_ref):
    c = c_ref[...]
    s = c * _sigmoid(c)
    o_ref[...] = _dot(s.astype(BF16), w_ref[...].astype(BF16)) + b_ref[...]


def _ada(c_all, w_ada, b_ada, l):
    bp = c_all.shape[0]
    n = w_ada.shape[-1]
    tn = 1024
    return pl.pallas_call(
        _ada_kernel, name="ada_mod",
        grid=(n // tn,),
        in_specs=[
            pl.BlockSpec((bp, D_MODEL), lambda j: (0, 0)),
            pl.BlockSpec((None, D_MODEL, tn), lambda j: (l, 0, j)),
            pl.BlockSpec((None, 1, tn), lambda j: (l, 0, j)),
        ],
        out_specs=pl.BlockSpec((bp, tn), lambda j: (0, j)),
        out_shape=jax.ShapeDtypeStruct((bp, n), F32),
        compiler_params=_cparams("arbitrary"),
    )(c_all, w_ada, b_ada.reshape(b_ada.shape[0], 1, n))


def _in_kernel(x_ref, g_ref, sc_ref, sh_ref, w_ref, sb_ref, o_ref, side_ref, hn_ref, *, nj):
    j = pl.program_id(1)

    @pl.when(j == 0)
    def _():
        y = _rms(x_ref[...], g_ref[...])
        hn_ref[...] = (y * (1.0 + sc_ref[0]) + sh_ref[0]).astype(BF16)

    acc = _dot(hn_ref[...], w_ref[...])
    o_ref[...] = acc.astype(BF16)

    @pl.when(j == nj - 1)
    def _():
        side_ref[...] = acc[:, IN_TN - SIDE_W:] + sb_ref[...]


def _in_proj(x2d, g, sc, sh, w_in_r, side_b, L):
    T = x2d.shape[0]
    tm = min(1024, L)
    nj = IN_COLS_R // IN_TN
    bidx = lambda i, j: ((i * tm) // L, 0, 0)
    return pl.pallas_call(
        functools.partial(_in_kernel, nj=nj), name="in_proj",
        grid=(T // tm, nj),
        in_specs=[
            pl.BlockSpec((tm, D_MODEL), lambda i, j: (i, 0)),
            pl.BlockSpec((1, D_MODEL), lambda i, j: (0, 0)),
            pl.BlockSpec((1, 1, D_MODEL), bidx),
            pl.BlockSpec((1, 1, D_MODEL), bidx),
            pl.BlockSpec((D_MODEL, IN_TN), lambda i, j: (0, j)),
            pl.BlockSpec((1, SIDE_W), lambda i, j: (0, 0)),
        ],
        out_specs=[
            pl.BlockSpec((tm, IN_TN), lambda i, j: (i, j)),
            pl.BlockSpec((tm, SIDE_W), lambda i, j: (i, 0)),
        ],
        out_shape=[
            jax.ShapeDtypeStruct((T, IN_COLS_R), BF16),
            jax.ShapeDtypeStruct((T, SIDE_W), F32),
        ],
        scratch_shapes=[pltpu.VMEM((tm, D_MODEL), BF16)],
        compiler_params=_cparams("arbitrary", "arbitrary"),
    )(x2d, g, sc, sh, w_in_r, side_b)


def _mlstm_chunk(q, k, v, li_col, lf_col, b_col, li_row, b_row, mask, C_ref, n_ref, m_ref, idx):
    m = m_ref[idx]
    C = C_ref[idx]
    n = n_ref[idx]
    Dm = jnp.where(mask, b_col - b_row + li_row, -jnp.inf)
    inter = b_col + m
    m_row = jnp.maximum(inter, jnp.max(Dm, axis=-1, keepdims=True))
    w_inter = jnp.exp(inter - m_row)
    s = _dot_nt(q, k) * jnp.exp(Dm - m_row)
    num = w_inter * _dot(q, C.astype(BF16)) + _dot(s.astype(BF16), v)
    qn = jnp.sum(q.astype(F32) * n, axis=-1, keepdims=True)
    den = w_inter * qn + jnp.sum(s, axis=-1, keepdims=True)
    h = num / jnp.maximum(jnp.abs(den), jnp.exp(-m_row))
    bL = jnp.sum(lf_col, axis=0, keepdims=True)
    g = bL - b_col + li_col
    m_new = jnp.maximum(bL + m, jnp.max(g, axis=0, keepdims=True))
    a = jnp.exp(bL + m - m_new)
    kw = k.astype(F32) * jnp.exp(g - m_new)
    C_ref[idx] = a * C + _dot(kw.T.astype(BF16), v)
    n_ref[idx] = a * n + jnp.sum(kw, axis=0, keepdims=True)
    m_ref[idx] = m_new
    return h


def _mlstm_kernel(qf_ref, kf_ref, vf_ref, sf_ref, qb_ref, kb_ref, vb_ref, sb_ref,
                  hf_ref, hb_ref, C_ref, n_ref, m_ref):
    c = pl.program_id(1)

    @pl.when(c == 0)
    def _():
        C_ref[...] = jnp.zeros_like(C_ref)
        n_ref[...] = jnp.zeros_like(n_ref)
        m_ref[...] = jnp.zeros_like(m_ref)

    row = lax.broadcasted_iota(I32, (ML_CHUNK, ML_CHUNK), 0)
    col = lax.broadcasted_iota(I32, (ML_CHUNK, ML_CHUNK), 1)
    lower = (col <= row)
    upper = (col >= row)
    tri_l = lower.astype(F32)
    tri_u = upper.astype(F32)
    kscale = ML_HEAD_DIM ** -0.5

    for d, (q_ref, k_ref, v_ref, s_ref, o_ref) in enumerate(
            ((qf_ref, kf_ref, vf_ref, sf_ref, hf_ref), (qb_ref, kb_ref, vb_ref, sb_ref, hb_ref))):
        gsub = s_ref[:, SIDE_GATE:SIDE_GATE + LANES]
        gT = gsub.T
        lsig = _log_sigmoid(gsub)
        lsigT = _log_sigmoid(gT)
        if d == 0:
            b_all = _dot(tri_l, lsig, precision=HI)
            b_allT = _dot(lsigT, tri_u, precision=HI)
            mask = lower
        else:
            b_all = _dot(tri_u, lsig, precision=HI)
            b_allT = _dot(lsigT, tri_l, precision=HI)
            mask = upper
        for h in range(ML_HEADS):
            ci = (2 * d) * ML_HEADS + h
            cf = (2 * d + 1) * ML_HEADS + h
            hs = slice(h * ML_HEAD_DIM, (h + 1) * ML_HEAD_DIM)
            q = q_ref[:, hs]
            k = k_ref[:, hs] * kscale
            v = v_ref[:, hs]
            hout = _mlstm_chunk(
                q, k, v,
                gsub[:, ci:ci + 1], lsig[:, cf:cf + 1], b_all[:, cf:cf + 1],
                gT[ci:ci + 1, :], b_allT[cf:cf + 1, :], mask,
                C_ref, n_ref, m_ref, d * ML_HEADS + h)
            o_ref[:, hs] = hout.astype(BF16)


def _mlstm(proj, side, B, L):
    T = B * L
    nc = L // ML_CHUNK
    fwd = lambda cb: (lambda b, c: (b * nc + c, cb))
    bwd = lambda cb: (lambda b, c: (b * nc + nc - 1 - c, cb))
    blk = lambda im: pl.BlockSpec((ML_CHUNK, ML_WIDTH), im)
    sblk = lambda im: pl.BlockSpec((ML_CHUNK, SIDE_W), im)
    nst = 2 * ML_HEADS
    return pl.pallas_call(
        _mlstm_kernel, name="mlstm",
        grid=(B, nc),
        in_specs=[blk(fwd(0)), blk(fwd(1)), blk(fwd(2)), sblk(fwd(0)),
                  blk(bwd(0)), blk(bwd(1)), blk(bwd(2)), sblk(bwd(0))],
        out_specs=[blk(fwd(0)), blk(bwd(0))],
        out_shape=[jax.ShapeDtypeStruct((T, ML_WIDTH), BF16)] * 2,
        scratch_shapes=[pltpu.VMEM((nst, ML_HEAD_DIM, ML_HEAD_DIM), F32),
                        pltpu.VMEM((nst, 1, ML_HEAD_DIM), F32),
                        pltpu.VMEM((nst, 1, 1), F32)],
        compiler_params=_cparams("arbitrary", "arbitrary"),
    )(proj, proj, proj, side, proj, proj, proj, side)


class _FftPlan:
    def __init__(self, L):
        self.L = L
        self.N = 2 * L
        self.N2 = FFT_N2
        self.N1 = self.N // self.N2
        self.N1h = self.N1 // 2
        self.K1 = self.N1h + 1
        self.K1p = -(-self.K1 // SUBLANES) * SUBLANES
        self.PT = self.N2 + FFT_TPAD
        self.P2 = 2 * self.N2 + FFT_KPAD


def _fft_tables(p):
    k1 = jnp.arange(p.K1p, dtype=I32)
    n1 = jnp.arange(p.N1h, dtype=I32)
    n2 = jnp.arange(p.N2, dtype=I32)
    n = p.N2 * n1[None, :] + n2[:, None]
    ph = (k1[None, :, None] * n[:, None, :]) % p.N
    ang = ph.astype(F32) * (2.0 * math.pi / p.N)
    valid = (k1 < p.K1)[None, :, None]
    c = jnp.where(valid, jnp.cos(ang), 0.0)
    s = jnp.where(valid, jnp.sin(ang), 0.0)
    t1 = jnp.concatenate([c, -s], axis=1).astype(BF16)
    wk = jnp.where((k1 == 0) | (k1 == p.N1h), 1.0, 2.0) / p.N
    wk = jnp.where(k1 < p.K1, wk, 0.0)[None, None, :]
    ct = jnp.swapaxes(c, 1, 2) * wk
    st = jnp.swapaxes(s, 1, 2) * wk
    t3 = jnp.concatenate([ct, -st], axis=2).astype(BF16)
    a2 = ((n2[:, None] * n2[None, :]) % p.N2).astype(F32) * (2.0 * math.pi / p.N2)
    c2, s2 = jnp.cos(a2), jnp.sin(a2)
    g2f = jnp.block([[c2, s2], [-s2, c2]]).astype(BF16)
    g2i = jnp.block([[c2, -s2], [s2, c2]]).astype(BF16)
    return t1, t3, g2f, g2i


def _pack_bf16_pair(lo, hi):
    lb = lax.bitcast_convert_type(lo.astype(BF16).astype(F32), U32)
    hb = lax.bitcast_convert_type(hi.astype(BF16).astype(F32), U32)
    return hb | (lb >> 16)


def _unpack_bf16_pair(w):
    lo = lax.bitcast_convert_type(w << 16, F32)
    hi = lax.bitcast_convert_type(w & jnp.uint32(0xFFFF0000), F32)
    return lo, hi


def _pack_lanes(x):
    return _pack_bf16_pair(x[:, :LANES], x[:, LANES:])


def _unpack_lanes(w):
    lo, hi = _unpack_bf16_pair(w)
    return jnp.concatenate([lo.astype(BF16), hi.astype(BF16)], axis=1)


def _fft_stage1(p, tst_ref, kd_ref, t1_ref):
    def body(n2, carry):
        xs = _unpack_lanes(tst_ref[pl.ds(n2, p.N1h, stride=p.PT), :])
        w = _pack_lanes(_dot(t1_ref[n2], xs))
        kd_ref[pl.ds(n2, p.K1p, stride=p.P2), :] = w[:p.K1p]
        kd_ref[pl.ds(p.N2 + n2, p.K1p, stride=p.P2), :] = w[p.K1p:]
        return carry
    lax.fori_loop(0, p.N2, body, 0, unroll=FFT_UNROLL_OUTER)


def _hyfeat_kernel(w1t_ref, w1c_ref, w1s_ref, b1_ref, w2_ref, b2_ref, fr_ref, o_ref, *, L, rows):
    i = pl.program_id(0)
    t = (lax.broadcasted_iota(I32, (rows, 1), 0) + i * rows).astype(F32) / L
    bands = (lax.broadcasted_iota(I32, (1, LANES), 1) + 1).astype(F32)
    ang = (2.0 * math.pi * t) * bands
    z = (t * w1t_ref[...] + _dot(jnp.cos(ang), w1c_ref[...], precision=HI)
         + _dot(jnp.sin(ang), w1s_ref[...], precision=HI) + b1_ref[...])
    h = jnp.sin(fr_ref[0:1, :] * z)
    h = jnp.sin(fr_ref[1:2, :] * (_dot(h, w2_ref[...], precision=HI) + b2_ref[...]))
    o_ref[...] = h


def _hyena_features(L, fw1, fb1, fw2, fb2, freq):
    rows = min(512, L)
    w1c = jnp.zeros((LANES, HY_FFN), F32).at[:HY_BANDS].set(fw1[1:1 + HY_BANDS])
    w1s = jnp.zeros((LANES, HY_FFN), F32).at[:HY_BANDS].set(fw1[1 + HY_BANDS:])
    full = lambda shp: pl.BlockSpec(shp, lambda i: (0,) * len(shp))
    return pl.pallas_call(
        functools.partial(_hyfeat_kernel, L=L, rows=rows), name="hy_feat",
        grid=(L // rows,),
        in_specs=[full((1, HY_FFN)), full((LANES, HY_FFN)), full((LANES, HY_FFN)), full((1, HY_FFN)),
                  full((HY_FFN, HY_FFN)), full((1, HY_FFN)), full((2, HY_FFN))],
        out_specs=pl.BlockSpec((rows, HY_FFN), lambda i: (i, 0)),
        out_shape=jax.ShapeDtypeStruct((L, HY_FFN), F32),
        compiler_params=_cparams("arbitrary"),
    )(fw1[0:1], w1c, w1s, fb1.reshape(1, -1), fw2, fb2.reshape(1, -1), freq)


def _hyfilt_kernel(h2_ref, wf_ref, wb_ref, df_ref, db_ref, skip_ref, t1_ref, g2_ref, o_ref,
                   tst_ref, kd_ref, *, p):
    L, N2 = p.L, p.N2
    decay_f = jnp.exp(df_ref[...])
    decay_b = jnp.exp(db_ref[...])
    w_fb = jnp.concatenate([wf_ref[...], wb_ref[...]], axis=1)

    def gen(n1, carry):
        r0 = pl.multiple_of(n1 * N2, N2)
        t = (lax.broadcasted_iota(I32, (N2, 1), 0) + r0).astype(F32) / L
        h2 = h2_ref[pl.ds(r0, N2), :]
        fb = _dot(h2, w_fb, precision=HI)
        f = fb[:, :LANES] * jnp.exp(-t * decay_f)
        b = fb[:, LANES:] * jnp.exp(-t * decay_b)
        b = jnp.where(t == 0.0, 0.0, b)
        s0 = pl.multiple_of(n1 * p.PT, SUBLANES)
        tst_ref[pl.ds(s0, N2), :] = _pack_bf16_pair(f, b)
        return carry + jnp.sum(jnp.abs(f) + jnp.abs(b), axis=0, keepdims=True)

    l1 = lax.fori_loop(0, p.N1h, gen, jnp.zeros((1, LANES), F32), unroll=2)
    inv = 1.0 / l1
    _fft_stage1(p, tst_ref, kd_ref, t1_ref)

    def spec(k1, carry):
        s0 = pl.multiple_of(k1 * p.P2, SUBLANES)
        x = _dot(g2_ref[...], _unpack_lanes(kd_ref[pl.ds(s0, 2 * N2), :]))
        xf, xb = x[:, :LANES], x[:, LANES:]
        kr = (xf[:N2] + xb[:N2]) * inv + skip_ref[...]
        ki = (xf[N2:] - xb[N2:]) * inv
        o_ref[k1] = jnp.concatenate([kr, ki], axis=0).astype(BF16)
        return carry
    lax.fori_loop(0, p.K1, spec, 0, unroll=FFT_UNROLL_MID)


def _hyena_spectra(p, h2, fw3, log_decay, skip, t1, g2f):
    nct = HY_WIDTH // LANES
    col = lambda d: (lambda o, c: (0, (2 * o + d) * nct + c))
    full = lambda shp: pl.BlockSpec(shp, lambda o, c: (0,) * len(shp))
    ld = log_decay.reshape(1, -1)
    return pl.pallas_call(
        functools.partial(_hyfilt_kernel, p=p), name="hy_spectra",
        grid=(HY_ORDER, nct),
        in_specs=[full((p.L, HY_FFN)),
                  pl.BlockSpec((HY_FFN, LANES), col(0)), pl.BlockSpec((HY_FFN, LANES), col(1)),
                  pl.BlockSpec((1, LANES), col(0)), pl.BlockSpec((1, LANES), col(1)),
                  pl.BlockSpec((None, 1, LANES), lambda o, c: (o, 0, c)),
                  full(t1.shape), full(g2f.shape)],
        out_specs=pl.BlockSpec((None, p.K1, 2 * p.N2, LANES), lambda o, c: (o, 0, 0, c)),
        out_shape=jax.ShapeDtypeStruct((HY_ORDER, p.K1, 2 * p.N2, HY_WIDTH), BF16),
        scratch_shapes=[pltpu.VMEM((p.N1h * p.PT, LANES), U32), pltpu.VMEM((p.K1p * p.P2, LANES), U32)],
        compiler_params=_cparams("arbitrary", "arbitrary"),
    )(h2, fw3, fw3, ld, ld, skip.reshape(HY_ORDER, 1, HY_WIDTH), t1, g2f)


def _conv3_chunk(u_ref, cw_ref, n1, nchunks, L):
    N2 = FFT_N2
    lanes = u_ref.shape[1]
    pk = 2 * SUBLANES
    r0 = pl.multiple_of(n1 * N2, N2)
    cur = u_ref[pl.ds(r0, N2), :].astype(F32)
    pstart = pl.multiple_of(jnp.maximum(r0 - pk, 0), pk)
    nstart = pl.multiple_of(jnp.minimum(r0 + N2, L - pk), pk)
    prev = u_ref[pl.ds(pstart, pk), :].astype(F32)[pk - 1:pk, :]
    nxt = u_ref[pl.ds(nstart, pk), :].astype(F32)[0:1, :]
    prev = prev * jnp.where(n1 > 0, 1.0, 0.0)
    nxt = nxt * jnp.where(n1 < nchunks - 1, 1.0, 0.0)
    row = lax.broadcasted_iota(I32, (N2, lanes), 0)
    up = jnp.where(row == 0, prev, pltpu.roll(cur, 1, axis=0))
    dn = jnp.where(row == N2 - 1, nxt, pltpu.roll(cur, N2 - 1, axis=0))
    return cw_ref[0:1, :] * up + cw_ref[1:2, :] * cur + cw_ref[2:3, :] * dn + cw_ref[3:4, :]


def _hyconv_kernel(ua_ref, ub_ref, cwa_ref, cwb_ref, sp_ref, t1_ref, t3_ref, g2f_ref, g2i_ref,
                   o_ref, tst_ref, kd_ref, *, p, pre_a):
    L, N2 = p.L, p.N2

    def chunk_a(n1):
        if pre_a:
            return _conv3_chunk(ua_ref, cwa_ref, n1, p.N1h, L)
        return ua_ref[pl.ds(pl.multiple_of(n1 * N2, N2), N2), :].astype(F32)

    def load(n1, carry):
        tst_ref[pl.ds(pl.multiple_of(n1 * p.PT, SUBLANES), N2), :] = _pack_lanes(chunk_a(n1))
        return carry
    lax.fori_loop(0, p.N1h, load, 0, unroll=2)

    _fft_stage1(p, tst_ref, kd_ref, t1_ref)

    def mid(k1, carry):
        s0 = pl.multiple_of(k1 * p.P2, SUBLANES)
        x = _dot(g2f_ref[...], _unpack_lanes(kd_ref[pl.ds(s0, 2 * N2), :]))
        xr, xi = x[:N2], x[N2:]
        kk = sp_ref[k1].astype(F32)
        kr, ki = kk[:N2], kk[N2:]
        y = jnp.concatenate([xr * kr - xi * ki, xr * ki + xi * kr], axis=0).astype(BF16)
        kd_ref[pl.ds(s0, 2 * N2), :] = _pack_lanes(_dot(g2i_ref[...], y))
        return carry
    lax.fori_loop(0, p.K1, mid, 0, unroll=FFT_UNROLL_MID)

    def stage3(n2, carry):
        br = kd_ref[pl.ds(n2, p.K1p, stride=p.P2), :]
        bi = kd_ref[pl.ds(N2 + n2, p.K1p, stride=p.P2), :]
        bc = _unpack_lanes(jnp.concatenate([br, bi], axis=0))
        tst_ref[pl.ds(n2, p.N1h, stride=p.PT), :] = _pack_lanes(_dot(t3_ref[n2], bc))
        return carry
    lax.fori_loop(0, N2, stage3, 0, unroll=FFT_UNROLL_OUTER)

    def store(n1, carry):
        lo, hi = _unpack_bf16_pair(tst_ref[pl.ds(pl.multiple_of(n1 * p.PT, SUBLANES), N2), :])
        y = jnp.concatenate([lo, hi], axis=1)
        g = _conv3_chunk(ub_ref, cwb_ref, n1, p.N1h, L)
        o_ref[pl.ds(pl.multiple_of(n1 * N2, N2), N2), :] = (g * y).astype(BF16)
        return carry
    lax.fori_loop(0, p.N1h, store, 0, unroll=2)


def _hyconv(p, a_arr, a_col, b_arr, b_col, cw_pack, spectra, order, tables, B, pre_a):
    L = p.L
    T = B * L
    nct = HY_WIDTH // HY_CT
    t1, t3, g2f, g2i = tables
    seq_bytes = L * HY_CT * 2
    const_bytes = 2 * (t1.size + t3.size + p.K1 * 2 * p.N2 * HY_CT)
    scratch_bytes = 4 * LANES * (p.N1h * p.PT + p.K1p * p.P2)
    single = 6 * seq_bytes + 2 * const_bytes + scratch_bytes > VMEM_LIMIT
    mode = dict(pipeline_mode=pl.Buffered(1)) if single else {}
    full = lambda shp: pl.BlockSpec(shp, lambda c, b: (0,) * len(shp), **mode)
    acb, bcb = a_col // HY_CT, b_col // HY_CT
    cwa_cb = (a_col - COL_HY) // HY_CT if pre_a else 0
    cwb_cb = (b_col - COL_HY) // HY_CT
    return pl.pallas_call(
        functools.partial(_hyconv_kernel, p=p, pre_a=pre_a), name="hy_conv",
        grid=(nct, B),
        in_specs=[pl.BlockSpec((L, HY_CT), lambda c, b: (b, acb + c), **mode),
                  pl.BlockSpec((L, HY_CT), lambda c, b: (b, bcb + c), **mode),
                  pl.BlockSpec((SUBLANES, HY_CT), lambda c, b: (0, cwa_cb + c)),
                  pl.BlockSpec((SUBLANES, HY_CT), lambda c, b: (0, cwb_cb + c)),
                  pl.BlockSpec((None, p.K1, 2 * p.N2, HY_CT), lambda c, b: (order, 0, 0, c), **mode),
                  full(t1.shape), full(t3.shape), full(g2f.shape), full(g2i.shape)],
        out_specs=pl.BlockSpec((L, HY_CT), lambda c, b: (b, c)),
        out_shape=jax.ShapeDtypeStruct((T, HY_WIDTH), BF16),
        scratch_shapes=[pltpu.VMEM((p.N1h * p.PT, LANES), U32), pltpu.VMEM((p.K1p * p.P2, LANES), U32)],
        compiler_params=_cparams("arbitrary", "arbitrary"),
    )(a_arr, b_arr, cw_pack, cw_pack, spectra, t1, t3, g2f, g2i)


def _mlaprep_kernel(cq_ref, ckv_ref, side_ref, cs_ref, gq_ref, gkv_ref, wuq_ref, wuk_ref, wuvt_ref,
                    q_ref, k_ref, vt_ref):
    cqn = _rms(cq_ref[...].astype(F32), gq_ref[...]).astype(BF16)
    ckvn = _rms(ckv_ref[...].astype(F32), gkv_ref[...]).astype(BF16)
    qa = _dot(cqn, wuq_ref[...])
    kn = _dot(ckvn, wuk_ref[...])
    vt = _dot_nt(wuvt_ref[...], ckvn)
    cs = cs_ref[...]
    cos2, sin2 = cs[:, :MLA_ROPE], cs[:, MLA_ROPE:]
    side = side_ref[...]
    k_rope = (side[:, :MLA_ROPE] * cos2 + side[:, MLA_ROPE:2 * MLA_ROPE] * sin2).astype(BF16)
    scale = MLA_QK ** -0.5 * math.log2(math.e)
    hw = MLA_NOPE + 2 * MLA_ROPE
    for h in range(MLA_HEADS):
        b0 = h * hw
        q_rope = qa[:, b0 + MLA_NOPE:b0 + MLA_QK] * cos2 + qa[:, b0 + MLA_QK:b0 + hw] * sin2
        q_ref[h, :, :MLA_NOPE] = (qa[:, b0:b0 + MLA_NOPE] * scale).astype(BF16)
        q_ref[h, :, MLA_NOPE:] = (q_rope * scale).astype(BF16)
        k_ref[h, :, :MLA_NOPE] = kn[:, h * MLA_NOPE:(h + 1) * MLA_NOPE].astype(BF16)
        k_ref[h, :, MLA_NOPE:] = k_rope
        vt_ref[h] = vt[h * MLA_V:(h + 1) * MLA_V, :].astype(BF16)


def _mla_prep(proj, side, cs, gq, gkv, wuq_ext, wuk, wuvt, B, L):
    T = B * L
    tm = min(512, L)
    nb = L // tm
    hw = MLA_NOPE + 2 * MLA_ROPE
    full = lambda shp: pl.BlockSpec(shp, lambda i: (0,) * len(shp))
    oidx = lambda i: (i // nb, 0, i % nb, 0)
    return pl.pallas_call(
        _mlaprep_kernel, name="mla_prep",
        grid=(T // tm,),
        in_specs=[pl.BlockSpec((tm, MLA_Q_RANK), lambda i: (i, COL_CQ // MLA_Q_RANK)),
                  pl.BlockSpec((tm, MLA_KV_RANK), lambda i: (i, COL_CKV // MLA_KV_RANK)),
                  pl.BlockSpec((tm, SIDE_W), lambda i: (i, 0)),
                  pl.BlockSpec((tm, 2 * MLA_ROPE), lambda i: (i % nb, 0)),
                  full((1, MLA_Q_RANK)), full((1, MLA_KV_RANK)),
                  full((MLA_Q_RANK, MLA_HEADS * hw)), full((MLA_KV_RANK, MLA_HEADS * MLA_NOPE)),
                  full((MLA_HEADS * MLA_V, MLA_KV_RANK))],
        out_specs=[pl.BlockSpec((None, MLA_HEADS, tm, MLA_QK), oidx),
                   pl.BlockSpec((None, MLA_HEADS, tm, MLA_QK), oidx),
                   pl.BlockSpec((None, MLA_HEADS, MLA_V, tm), lambda i: (i // nb, 0, 0, i % nb))],
        out_shape=[jax.ShapeDtypeStruct((B, MLA_HEADS, L, MLA_QK), BF16),
                   jax.ShapeDtypeStruct((B, MLA_HEADS, L, MLA_QK), BF16),
                   jax.ShapeDtypeStruct((B, MLA_HEADS, MLA_V, L), BF16)],
        compiler_params=_cparams("arbitrary"),
    )(proj, proj, side, cs, gq, gkv, wuq_ext, wuk, wuvt)


def _flash_kernel(q_ref, k_ref, vt_ref, o_ref, *, tq, tk, nsplit):
    L = k_ref.shape[0]
    sub = tq // nsplit
    qs = [q_ref[pl.ds(i * sub, sub), :] for i in range(nsplit)]

    def step(j, carry):
        k0 = pl.multiple_of(j * tk, tk)
        kj = k_ref[pl.ds(k0, tk), :]
        vtj = vt_ref[:, pl.ds(k0, tk)]
        out = []
        for i in range(nsplit):
            m_prev, l_prev, acc = carry[i]
            st = _dot_nt(kj, qs[i])
            m_new = jnp.maximum(m_prev, jnp.max(st, axis=0, keepdims=True))
            alpha = jnp.exp2(m_prev - m_new)
            pt = jnp.exp2(st - m_new)
            l_new = alpha * l_prev + jnp.sum(pt, axis=0, keepdims=True)
            acc = alpha * acc + _dot(vtj, pt.astype(BF16))
            out.append((m_new, l_new, acc))
        return tuple(out)

    init = tuple((jnp.full((1, sub), -jnp.inf, F32), jnp.zeros((1, sub), F32), jnp.zeros((MLA_V, sub), F32))
                 for _ in range(nsplit))
    res = lax.fori_loop(0, L // tk, step, init)
    for i in range(nsplit):
        _, l_fin, acc = res[i]
        o_ref[pl.ds(i * sub, sub), :] = (acc / l_fin).T.astype(BF16)


def _flash(q, k, v, B, L):
    tq = min(ATTN_TQ, L)
    tk = min(ATTN_TK, L)
    nq = L // tq
    return pl.pallas_call(
        functools.partial(_flash_kernel, tq=tq, tk=tk, nsplit=ATTN_SPLIT), name="mla_attn",
        grid=(B, MLA_HEADS, nq),
        in_specs=[pl.BlockSpec((None, None, tq, MLA_QK), lambda b, h, i: (b, h, i, 0)),
                  pl.BlockSpec((None, None, L, MLA_QK), lambda b, h, i: (b, h, 0, 0)),
                  pl.BlockSpec((None, None, MLA_V, L), lambda b, h, i: (b, h, 0, 0))],
        out_specs=pl.BlockSpec((tq, MLA_V), lambda b, h, i: (b * nq + i, h)),
        out_shape=jax.ShapeDtypeStruct((B * L, MLA_WIDTH), BF16),
        compiler_params=_cparams("arbitrary", "arbitrary", "arbitrary"),
    )(q, k, v)


def _merge_kernel(hf_ref, hb_ref, mlo_ref, mlg_ref, ohy_ref, omla_ref, gml_ref, ghy_ref, gmla_ref,
                  wml_ref, why_ref, wmla_ref, o_ref, oml_ref):
    @pl.when(pl.program_id(1) == 0)
    def _():
        for h in range(ML_HEADS):
            hs = slice(h * ML_HEAD_DIM, (h + 1) * ML_HEAD_DIM)
            x = hf_ref[:, hs].astype(F32) + hb_ref[:, hs].astype(F32)
            y = _rms(x, mlg_ref[:, hs])
            oml_ref[:, hs] = (y * _sigmoid(mlo_ref[:, hs].astype(F32))).astype(BF16)

    acc = _sigmoid(gml_ref[...].astype(F32)) * _dot(oml_ref[...], wml_ref[...])
    acc += _sigmoid(ghy_ref[...].astype(F32)) * _dot(ohy_ref[...], why_ref[...])
    acc += _sigmoid(gmla_ref[...].astype(F32)) * _dot(omla_ref[...], wmla_ref[...])
    o_ref[...] = acc.astype(BF16)


def _merge(hf, hb, proj, ml_norm_g, o_hy, o_mla, wml, why, wmla, L):
    T = hf.shape[0]
    tm = min(1024, L)
    tn = 512
    row = lambda cb: (lambda i, j: (i, cb))
    gate = lambda k: (lambda i, j: (i, (COL_BRG + k * D_MODEL) // tn + j))
    wsp = pl.BlockSpec((ML_WIDTH, tn), lambda i, j: (0, j))
    return pl.pallas_call(
        _merge_kernel, name="merge",
        grid=(T // tm, D_MODEL // tn),
        in_specs=[pl.BlockSpec((tm, ML_WIDTH), row(0)), pl.BlockSpec((tm, ML_WIDTH), row(0)),
                  pl.BlockSpec((tm, ML_WIDTH), row(3)),
                  pl.BlockSpec((1, ML_WIDTH), lambda i, j: (0, 0)),
                  pl.BlockSpec((tm, HY_WIDTH), row(0)), pl.BlockSpec((tm, MLA_WIDTH), row(0)),
                  pl.BlockSpec((tm, tn), gate(0)), pl.BlockSpec((tm, tn), gate(1)), pl.BlockSpec((tm, tn), gate(2)),
                  wsp, wsp, wsp],
        out_specs=pl.BlockSpec((tm, tn), lambda i, j: (i, j)),
        out_shape=jax.ShapeDtypeStruct((T, D_MODEL), BF16),
        scratch_shapes=[pltpu.VMEM((tm, ML_WIDTH), BF16)],
        compiler_params=_cparams("arbitrary", "arbitrary"),
    )(hf, hb, proj, ml_norm_g, o_hy, o_mla, proj, proj, proj, wml, why, wmla)


def _outproj_kernel(m_ref, w_ref, x_ref, g_ref, o_ref):
    o_ref[...] = x_ref[...] + g_ref[0] * _dot(m_ref[...], w_ref[...])


def _out_proj(merged, w_out, x2d, g1, L):
    T = x2d.shape[0]
    tm = min(1024, L)
    tn = 512
    return pl.pallas_call(
        _outproj_kernel, name="out_proj",
        grid=(T // tm, D_MODEL // tn),
        in_specs=[pl.BlockSpec((tm, D_MODEL), lambda i, j: (i, 0)),
                  pl.BlockSpec((D_MODEL, tn), lambda i, j: (0, j)),
                  pl.BlockSpec((tm, tn), lambda i, j: (i, j)),
                  pl.BlockSpec((1, 1, tn), lambda i, j: ((i * tm) // L, 0, j))],
        out_specs=pl.BlockSpec((tm, tn), lambda i, j: (i, j)),
        out_shape=jax.ShapeDtypeStruct((T, D_MODEL), F32),
        compiler_params=_cparams("arbitrary", "arbitrary"),
    )(merged, w_out, x2d, g1)


def _route_kernel(x_ref, g_ref, sc_ref, sh_ref, wr_ref, rb_ref, hp_ref, idx_ref, wgt_ref, lg_ref):
    half = D_MODEL // 2
    nsb = x_ref.shape[0] // LANES
    hn = _rms(x_ref[...], g_ref[...]) * (1.0 + sc_ref[0]) + sh_ref[0]
    hp_ref[...] = _pack_bf16_pair(hn[:, :half], hn[:, half:])
    for sb in range(nsb):
        logits = _dot_nt(wr_ref[...], hn[sb * LANES:(sb + 1) * LANES, :], precision=HI)
        lg_ref[pl.ds(sb * N_EXPERTS, N_EXPERTS), :] = _sigmoid(logits)
    srow = [lg_ref[pl.ds(e, nsb, stride=N_EXPERTS), :] for e in range(N_EXPERTS)]
    rows = [srow[e] + rb_ref[e:e + 1, :] for e in range(N_EXPERTS)]
    gs = []
    for g in range(N_GROUPS):
        r = rows[g * EXPERTS_PER_GROUP:(g + 1) * EXPERTS_PER_GROUP]
        best = None
        for a in range(EXPERTS_PER_GROUP):
            for b in range(a + 1, EXPERTS_PER_GROUP):
                s = r[a] + r[b]
                best = s if best is None else jnp.maximum(best, s)
        gs.append(best)
    gsel = jnp.zeros_like(gs[0], dtype=I32)
    gbest = gs[0]
    for g in range(1, N_GROUPS):
        better = gs[g] > gbest
        gsel = jnp.where(better, g, gsel)
        gbest = jnp.where(better, gs[g], gbest)
    masked = [jnp.where(gsel == (e // EXPERTS_PER_GROUP), rows[e], -jnp.inf) for e in range(N_EXPERTS)]
    picks = []
    for _ in range(2):
        bi = jnp.zeros_like(gsel)
        bv = masked[0]
        bs = srow[0]
        for e in range(1, N_EXPERTS):
            better = masked[e] > bv
            bi = jnp.where(better, e, bi)
            bv = jnp.where(better, masked[e], bv)
            bs = jnp.where(better, srow[e], bs)
        picks.append((bi, bs))
        masked = [jnp.where(bi == e, -jnp.inf, masked[e]) for e in range(N_EXPERTS)]
    (i1, s1), (i2, s2) = picks
    tot = s1 + s2
    idx_ref[0] = i1
    idx_ref[1] = i2
    wgt_ref[0] = s1 / tot
    wgt_ref[1] = s2 / tot


def _route(x2d, g, sc, sh, w_router, router_bias, L):
    T = x2d.shape[0]
    tm = min(ROUTE_TM, L)
    nsb = tm // LANES
    bidx = lambda i: ((i * tm) // L, 0, 0)
    rb = jnp.broadcast_to(router_bias.reshape(N_EXPERTS, 1), (N_EXPERTS, LANES))
    hp, idx, wgt = _route_call(x2d, g, sc, sh, w_router, rb, T, tm, nsb, bidx)
    return hp, idx.reshape(2, T), wgt.reshape(2, T)


def _route_call(x2d, g, sc, sh, w_router, rb, T, tm, nsb, bidx):
    return pl.pallas_call(
        _route_kernel, name="route",
        grid=(T // tm,),
        in_specs=[pl.BlockSpec((tm, D_MODEL), lambda i: (i, 0)),
                  pl.BlockSpec((1, D_MODEL), lambda i: (0, 0)),
                  pl.BlockSpec((1, 1, D_MODEL), bidx), pl.BlockSpec((1, 1, D_MODEL), bidx),
                  pl.BlockSpec((N_EXPERTS, D_MODEL), lambda i: (0, 0)),
                  pl.BlockSpec((N_EXPERTS, LANES), lambda i: (0, 0))],
        out_specs=[pl.BlockSpec((tm, D_MODEL // 2), lambda i: (i, 0)),
                   pl.BlockSpec((2, nsb, LANES), lambda i: (0, i, 0)),
                   pl.BlockSpec((2, nsb, LANES), lambda i: (0, i, 0))],
        out_shape=[jax.ShapeDtypeStruct((T, D_MODEL // 2), U32),
                   jax.ShapeDtypeStruct((2, T // LANES, LANES), I32),
                   jax.ShapeDtypeStruct((2, T // LANES, LANES), F32)],
        scratch_shapes=[pltpu.VMEM((nsb * N_EXPERTS, LANES), F32)],
        compiler_params=_cparams("arbitrary"),
    )(x2d, g, sc, sh, w_router.T, rb)


def _moe_plan(idx, T):
    A = 2 * T
    e = idx.reshape(A)
    tok = jnp.tile(jnp.arange(T, dtype=I32), 2)
    onehot = (e[:, None] == jnp.arange(N_EXPERTS, dtype=I32)[None, :]).astype(I32)
    csum = jnp.cumsum(onehot, axis=0)
    counts = csum[-1]
    rank = jnp.sum(csum * onehot, axis=1) - 1
    padded = ((counts + MOE_TILE - 1) // MOE_TILE) * MOE_TILE
    ends = jnp.cumsum(padded)
    starts = ends - padded
    pos = jnp.sum(starts[None, :] * onehot, axis=1) + rank
    R = A + N_EXPERTS * MOE_TILE
    src = jnp.zeros((R,), I32).at[pos].set(tok, unique_indices=True)
    n_tiles = R // MOE_TILE
    tile_start = jnp.arange(n_tiles, dtype=I32) * MOE_TILE
    tile_e = jnp.minimum(jnp.sum((tile_start[:, None] >= ends[None, :]).astype(I32), axis=1), N_EXPERTS - 1)
    num_used = (ends[-1] // MOE_TILE).astype(I32).reshape(1)
    return src, pos.reshape(2, T).astype(I32), tile_e.astype(I32), num_used


def _row_gather_copy(src_hbm, dst_ref, sem, src_row, dst_row):
    return pltpu.make_async_copy(src_hbm.at[pl.ds(src_row, 1)], dst_ref.at[pl.ds(dst_row, 1)], sem)


def _issue_row_gathers(src_hbm, idx_ref, dst_ref, sem, rows, dst_base=0):
    for r in range(rows):
        _row_gather_copy(src_hbm, dst_ref, sem, idx_ref[0, 0, r], dst_base + r).start()


def _wait_row_gathers(src_hbm, dst_ref, sem):
    pltpu.make_async_copy(src_hbm.at[pl.ds(0, dst_ref.shape[0])], dst_ref, sem).wait()


def _expert_kernel(te_ref, nt_ref, idx_ref, idxn_ref, hp_hbm, wgu_ref, wd_ref, o_ref, xb0_ref, xb1_ref, sems):
    t = pl.program_id(0)
    nt = nt_ref[0]
    half = D_MODEL // 2

    @pl.when(t == 0)
    def _():
        _issue_row_gathers(hp_hbm, idx_ref, xb0_ref, sems.at[0], MOE_TILE)

    def run(cur_ref, nxt_ref, cur_sem, nxt_sem):
        _wait_row_gathers(hp_hbm, cur_ref, cur_sem)
        _issue_row_gathers(hp_hbm, idxn_ref, nxt_ref, nxt_sem, MOE_TILE)
        lo, hi = _unpack_bf16_pair(cur_ref[...])
        h1 = _dot(lo.astype(BF16), wgu_ref[:half, :]) + _dot(hi.astype(BF16), wgu_ref[half:, :])
        a, b = h1[:, :D_EXPERT], h1[:, D_EXPERT:]
        act = (a * _sigmoid(a) * b).astype(BF16)
        y = _dot(act, wd_ref[...])
        o_ref[...] = _pack_bf16_pair(y[:, :half], y[:, half:])

        @pl.when(t == nt - 1)
        def _():
            _wait_row_gathers(hp_hbm, nxt_ref, nxt_sem)

    even = lax.rem(t, 2) == 0

    @pl.when(jnp.logical_and(t < nt, even))
    def _():
        run(xb0_ref, xb1_ref, sems.at[0], sems.at[1])

    @pl.when(jnp.logical_and(t < nt, jnp.logical_not(even)))
    def _():
        run(xb1_ref, xb0_ref, sems.at[1], sems.at[0])

    @pl.when(t >= nt)
    def _():
        o_ref[...] = jnp.zeros_like(o_ref)


def _experts(hp, src, tile_e, num_used, wgu, wd, l):
    R = src.shape[0]
    nt = R // MOE_TILE
    half = D_MODEL // 2
    src3 = src.reshape(nt, 1, MOE_TILE)
    grid_spec = pltpu.PrefetchScalarGridSpec(
        num_scalar_prefetch=2,
        grid=(nt,),
        in_specs=[pl.BlockSpec((1, 1, MOE_TILE), lambda t, te, n: (t, 0, 0), memory_space=pltpu.SMEM),
                  pl.BlockSpec((1, 1, MOE_TILE), lambda t, te, n: (jnp.minimum(t + 1, n[0] - 1), 0, 0),
                               memory_space=pltpu.SMEM),
                  pl.BlockSpec(memory_space=pl.ANY),
                  pl.BlockSpec((None, None, D_MODEL, 2 * D_EXPERT), lambda t, te, n: (l, te[t], 0, 0)),
                  pl.BlockSpec((None, None, D_EXPERT, D_MODEL), lambda t, te, n: (l, te[t], 0, 0))],
        out_specs=pl.BlockSpec((MOE_TILE, half), lambda t, te, n: (t, 0)),
        scratch_shapes=[pltpu.VMEM((MOE_TILE, half), U32), pltpu.VMEM((MOE_TILE, half), U32),
                        pltpu.SemaphoreType.DMA((2,))],
    )
    return pl.pallas_call(
        _expert_kernel, name="moe_experts",
        grid_spec=grid_spec,
        out_shape=jax.ShapeDtypeStruct((R, half), U32),
        compiler_params=_cparams("arbitrary"),
    )(tile_e, num_used, src3, src3, hp, wgu, wd)


def _combine_kernel(p1_ref, p2_ref, p1n_ref, p2n_ref, ys_hbm, w_ref, x_ref, g_ref, fg_ref, o_ref,
                    buf0_ref, buf1_ref, sems, *, rows, final):
    i = pl.program_id(0)
    n = pl.num_programs(0)
    half = D_MODEL // 2

    @pl.when(i == 0)
    def _():
        _issue_row_gathers(ys_hbm, p1_ref, buf0_ref, sems.at[0], rows)
        _issue_row_gathers(ys_hbm, p2_ref, buf0_ref, sems.at[0], rows, rows)

    def run(cur_ref, nxt_ref, cur_sem, nxt_sem):
        _wait_row_gathers(ys_hbm, cur_ref, cur_sem)
        _issue_row_gathers(ys_hbm, p1n_ref, nxt_ref, nxt_sem, rows)
        _issue_row_gathers(ys_hbm, p2n_ref, nxt_ref, nxt_sem, rows, rows)

        w = w_ref[...]
        w1, w2 = w[:, 0:1], w[:, 1:2]
        lo1, hi1 = _unpack_bf16_pair(cur_ref[pl.ds(0, rows), :])
        lo2, hi2 = _unpack_bf16_pair(cur_ref[pl.ds(rows, rows), :])
        g = g_ref[0]
        xlo = x_ref[:, :half] + g[:, :half] * (w1 * lo1 + w2 * lo2)
        xhi = x_ref[:, half:] + g[:, half:] * (w1 * hi1 + w2 * hi2)
        if final:
            ms = (jnp.sum(xlo * xlo, axis=-1, keepdims=True)
                  + jnp.sum(xhi * xhi, axis=-1, keepdims=True)) / D_MODEL
            r = lax.rsqrt(ms + EPS)
            xlo = xlo * r * fg_ref[:, :half]
            xhi = xhi * r * fg_ref[:, half:]
        o_ref[:, :half] = xlo
        o_ref[:, half:] = xhi

        @pl.when(i == n - 1)
        def _():
            _wait_row_gathers(ys_hbm, nxt_ref, nxt_sem)

    even = lax.rem(i, 2) == 0

    @pl.when(even)
    def _():
        run(buf0_ref, buf1_ref, sems.at[0], sems.at[1])

    @pl.when(jnp.logical_not(even))
    def _():
        run(buf1_ref, buf0_ref, sems.at[1], sems.at[0])


def _combine(ys, pos, wgt, x2d, g2, final_g, L, final):
    T = x2d.shape[0]
    rows = GATHER_TILE
    nt = T // rows
    half = D_MODEL // 2
    cur = lambda: pl.BlockSpec((1, 1, rows), lambda i: (i, 0, 0), memory_space=pltpu.SMEM)
    nxt = lambda: pl.BlockSpec((1, 1, rows), lambda i: (jnp.minimum(i + 1, nt - 1), 0, 0), memory_space=pltpu.SMEM)
    p1 = pos[0].reshape(nt, 1, rows)
    p2 = pos[1].reshape(nt, 1, rows)
    return pl.pallas_call(
        functools.partial(_combine_kernel, rows=rows, final=final), name="moe_combine",
        grid=(nt,),
        in_specs=[cur(), cur(), nxt(), nxt(),
                  pl.BlockSpec(memory_space=pl.ANY),
                  pl.BlockSpec((rows, 2), lambda i: (i, 0)),
                  pl.BlockSpec((rows, D_MODEL), lambda i: (i, 0)),
                  pl.BlockSpec((1, 1, D_MODEL), lambda i: ((i * rows) // L, 0, 0)),
                  pl.BlockSpec((1, D_MODEL), lambda i: (0, 0))],
        out_specs=pl.BlockSpec((rows, D_MODEL), lambda i: (i, 0)),
        out_shape=jax.ShapeDtypeStruct((T, D_MODEL), F32),
        scratch_shapes=[pltpu.VMEM((2 * rows, half), U32), pltpu.VMEM((2 * rows, half), U32),
                        pltpu.SemaphoreType.DMA((2,))],
        compiler_params=_cparams("arbitrary"),
    )(p1, p2, p1, p2, ys, wgt.T, x2d, g2, final_g)


def _swap_halves(w):
    h = w.shape[-1] // 2
    return jnp.concatenate([w[..., h:], w[..., :h]], axis=-1)


def _prep_layer(P, l):
    w = P["w_in"][l]
    o_g = 4 * ML_WIDTH
    o_hy = o_g + 4 * ML_HEADS
    o_cq = o_hy + 3 * HY_WIDTH
    o_ckv = o_cq + MLA_Q_RANK
    o_kr = o_ckv + MLA_KV_RANK
    o_br = o_kr + MLA_ROPE
    kr = w[:, o_kr:o_br]
    side_pad = jnp.zeros((D_MODEL, SIDE_W - SIDE_GATE - 4 * ML_HEADS), F32)
    w_in_r = jnp.concatenate(
        [w[:, :o_g], w[:, o_hy:o_cq], w[:, o_br:], w[:, o_cq:o_ckv], w[:, o_ckv:o_kr],
         kr, _swap_halves(kr), w[:, o_g:o_hy], side_pad], axis=1).astype(BF16)
    side_b = jnp.zeros((1, SIDE_W), F32).at[0, SIDE_GATE:SIDE_GATE + 4 * ML_HEADS].set(P["ml_gate_b"][l].reshape(-1))
    wuq = P["w_uq"][l].reshape(MLA_Q_RANK, MLA_HEADS, MLA_QK)
    qr = wuq[..., MLA_NOPE:]
    wuq_ext = jnp.concatenate([wuq, _swap_halves(qr)], axis=-1).reshape(MLA_Q_RANK, -1).astype(BF16)
    wukv = P["w_ukv"][l].reshape(MLA_KV_RANK, MLA_HEADS, MLA_NOPE + MLA_V)
    cw = jnp.concatenate([P["hy_conv_w"][l], P["hy_conv_b"][l][None, :],
                          jnp.zeros((SUBLANES - 4, 3 * HY_WIDTH), F32)], axis=0)
    return dict(
        w_in_r=w_in_r, side_b=side_b, wuq_ext=wuq_ext, cw=cw,
        wuk=wukv[..., :MLA_NOPE].reshape(MLA_KV_RANK, -1).astype(BF16),
        wuvt=wukv[..., MLA_NOPE:].reshape(MLA_KV_RANK, -1).T.astype(BF16),
        wml=P["w_br_ml"][l].astype(BF16), why=P["w_br_hy"][l].astype(BF16), wmla=P["w_br_mla"][l].astype(BF16),
        w_out=P["w_out"][l].astype(BF16),
        norm1_g=P["norm1_g"][l].reshape(1, -1), norm2_g=P["norm2_g"][l].reshape(1, -1),
        ml_norm_g=P["ml_norm_g"][l].reshape(1, -1),
        gq=P["mla_q_norm_g"][l].reshape(1, -1), gkv=P["mla_kv_norm_g"][l].reshape(1, -1),
    )


def _rope_table(L):
    pos = jnp.arange(L, dtype=F32)
    inv = ROPE_BASE ** (-jnp.arange(0, MLA_ROPE, 2, dtype=F32) / MLA_ROPE)
    ang = pos[:, None] * inv[None, :]
    cos, sin = jnp.cos(ang), jnp.sin(ang)
    return jnp.concatenate([cos, cos, -sin, sin], axis=-1)


def _encoder(x, mods, P, W, wgu_bf, wd_bf):
    B, L, _ = x.shape
    T = B * L
    x2d = x.reshape(T, D_MODEL)
    plan = _FftPlan(L)
    tables = _fft_tables(plan)
    cs = _rope_table(L)
    depth = len(W)
    for l in range(depth):
        wl = W[l]
        sh1, sc1, g1, sh2, sc2, g2 = [mods[l][:, k].reshape(B, 1, D_MODEL) for k in range(6)]
        proj, side = _in_proj(x2d, wl["norm1_g"], sc1, sh1, wl["w_in_r"], wl["side_b"], L)
        hf, hb = _mlstm(proj, side, B, L)
        h2 = _hyena_features(L, P["hy_fw1"][l], P["hy_fb1"][l], P["hy_fw2"][l], P["hy_fb2"][l], P["hy_freq"][l])
        spectra = _hyena_spectra(plan, h2, P["hy_fw3"][l], P["hy_log_decay"][l], P["hy_skip"][l], tables[0], tables[2])
        z = _hyconv(plan, proj, COL_HY, proj, COL_HY + HY_WIDTH, wl["cw"], spectra, 0, tables, B, True)
        o_hy = _hyconv(plan, z, 0, proj, COL_HY + 2 * HY_WIDTH, wl["cw"], spectra, 1, tables, B, False)
        q, k, v = _mla_prep(proj, side, cs, wl["gq"], wl["gkv"], wl["wuq_ext"], wl["wuk"], wl["wuvt"], B, L)
        o_mla = _flash(q, k, v, B, L)
        merged = _merge(hf, hb, proj, wl["ml_norm_g"], o_hy, o_mla, wl["wml"], wl["why"], wl["wmla"], L)
        x2d = _out_proj(merged, wl["w_out"], x2d, g1, L)
        hp, idx, wgt = _route(x2d, wl["norm2_g"], sc2, sh2, P["w_router"], P["router_bias"], L)
        src, pos, tile_e, num_used = _moe_plan(idx, T)
        ys = _experts(hp, src, tile_e, num_used, wgu_bf, wd_bf, l)
        x2d = _combine(ys, pos, wgt, x2d, g2, P["final_g"].reshape(1, -1), L, final=(l == depth - 1))
    return x2d.reshape(B, L, D_MODEL)


def kernel(x_prompt, x_sample, c_prompt, c_sample, w_ada, b_ada, norm1_g, norm2_g, w_in, ml_gate_b, ml_norm_g,
           hy_conv_w, hy_conv_b, hy_fw1, hy_fb1, hy_fw2, hy_fb2, hy_fw3, hy_freq, hy_log_decay, hy_skip,
           mla_q_norm_g, mla_kv_norm_g, w_uq, w_ukv, w_br_ml, w_br_hy, w_br_mla, w_out, w_router, router_bias,
           w_gate_up, w_down, final_g):
    P = dict(w_ada=w_ada, b_ada=b_ada, norm1_g=norm1_g, norm2_g=norm2_g, w_in=w_in, ml_gate_b=ml_gate_b,
             ml_norm_g=ml_norm_g, hy_conv_w=hy_conv_w, hy_conv_b=hy_conv_b, hy_fw1=hy_fw1, hy_fb1=hy_fb1,
             hy_fw2=hy_fw2, hy_fb2=hy_fb2, hy_fw3=hy_fw3, hy_freq=hy_freq, hy_log_decay=hy_log_decay,
             hy_skip=hy_skip, mla_q_norm_g=mla_q_norm_g, mla_kv_norm_g=mla_kv_norm_g, w_uq=w_uq, w_ukv=w_ukv,
             w_br_ml=w_br_ml, w_br_hy=w_br_hy, w_br_mla=w_br_mla, w_out=w_out, w_router=w_router,
             router_bias=router_bias, w_gate_up=w_gate_up, w_down=w_down, final_g=final_g)
    depth = w_in.shape[0]
    W = [_prep_layer(P, l) for l in range(depth)]
    wgu_bf = w_gate_up.astype(BF16)
    wd_bf = w_down.astype(BF16)
    bp, bs = c_prompt.shape[0], c_sample.shape[0]
    rows = -(-(bp + bs) // SUBLANES) * SUBLANES
    c_all = jnp.zeros((rows, D_MODEL), F32).at[:bp].set(c_prompt).at[bp:bp + bs].set(c_sample)
    mods_p, mods_s = [], []
    for l in range(depth):
        mod = _ada(c_all, w_ada, b_ada, l).reshape(rows, 6, D_MODEL)
        mods_p.append(mod[:bp])
        mods_s.append(mod[bp:bp + bs])
    y_prompt = _encoder(x_prompt, mods_p, P, W, wgu_bf, wd_bf)
    y_sample = _encoder(x_sample, mods_s, P, W, wgu_bf, wd_bf)
    return (y_prompt, y_sample)
```

```python
import functools
import math

import jax
import jax.numpy as jnp
from jax import lax
from jax.experimental import pallas as pl
from jax.experimental.pallas import tpu as pltpu

F32 = jnp.float32
BF16 = jnp.bfloat16
U32 = jnp.uint32
I32 = jnp.int32

D_MODEL = 2048
ML_HEADS = 4
ML_HEAD_DIM = 256
ML_WIDTH = ML_HEADS * ML_HEAD_DIM
ML_CHUNK = 128
HY_WIDTH = 1024
HY_ORDER = 2
HY_BANDS = 16
HY_FFN = 64
MLA_HEADS = 8
MLA_Q_RANK = 512
MLA_KV_RANK = 256
MLA_NOPE = 128
MLA_ROPE = 64
MLA_V = 128
MLA_VE = MLA_V + 16
MLA_WIDTH = MLA_HEADS * MLA_V
MLA_QK = MLA_NOPE + MLA_ROPE
ROPE_BASE = 10000.0
N_EXPERTS = 16
N_GROUPS = 4
EXPERTS_PER_GROUP = N_EXPERTS // N_GROUPS
D_EXPERT = 1024
EPS = 1e-6

LANES = 128
SUBLANES = 8
VMEM_LIMIT = 56 * 1024 * 1024

COL_ML = 0
COL_HY = 4 * ML_WIDTH
COL_BRG = COL_HY + 3 * HY_WIDTH
COL_CQ = COL_BRG + 3 * D_MODEL
COL_CKV = COL_CQ + MLA_Q_RANK
COL_SIDE = COL_CKV + MLA_KV_RANK
SIDE_W = 256
SIDE_GATE = 128
IN_COLS_R = COL_SIDE + SIDE_W
IN_TN = 1024

FFT_N2 = 128
FFT_TPAD = SUBLANES
FFT_KPAD = SUBLANES
FFT_UNROLL_OUTER = 16
FFT_UNROLL_MID = 8
HY_CT = 2 * LANES

MOE_TILE = 256
GATHER_TILE = 256
ROUTE_TM = 1024
ATTN_TQ = 2048
ATTN_TK = 2048
ATTN_SPLIT = 2


def _cparams(*sem):
    return pltpu.CompilerParams(dimension_semantics=sem, vmem_limit_bytes=VMEM_LIMIT)


def _rms(x, g):
    return x * lax.rsqrt(jnp.mean(x * x, axis=-1, keepdims=True) + EPS) * g


def _log_sigmoid(x):
    return -(jnp.maximum(-x, 0.0) + jnp.log1p(jnp.exp(-jnp.abs(x))))


def _sigmoid(x):
    return 1.0 / (1.0 + jnp.exp(-x))


def _dot(a, b, **kw):
    return jnp.dot(a, b, preferred_element_type=F32, **kw)


def _dot_nt(a, b, **kw):
    return lax.dot_general(a, b, (((1,), (1,)), ((), ())), preferred_element_type=F32, **kw)


HI = lax.Precision.HIGHEST


def _ada_kernel(c_ref, w_ref, b_ref, o_ref):
    c = c_ref[...]
    s = c * _sigmoid(c)
    o_ref[...] = _dot(s.astype(BF16), w_ref[...].astype(BF16)) + b_ref[...]


def _ada(c_all, w_ada, b_ada, l):
    bp = c_all.shape[0]
    n = w_ada.shape[-1]
    tn = 1024
    return pl.pallas_call(
        _ada_kernel, name="ada_mod",
        grid=(n // tn,),
        in_specs=[
            pl.BlockSpec((bp, D_MODEL), lambda j: (0, 0)),
            pl.BlockSpec((None, D_MODEL, tn), lambda j: (l, 0, j)),
            pl.BlockSpec((None, 1, tn), lambda j: (l, 0, j)),
        ],
        out_specs=pl.BlockSpec((bp, tn), lambda j: (0, j)),
        out_shape=jax.ShapeDtypeStruct((bp, n), F32),
        compiler_params=_cparams("arbitrary"),
    )(c_all, w_ada, b_ada.reshape(b_ada.shape[0], 1, n))


def _in_kernel(x_ref, g_ref, sc_ref, sh_ref, w_ref, sb_ref, o_ref, side_ref, hn_ref, *, nj):
    j = pl.program_id(1)

    @pl.when(j == 0)
    def _():
        y = _rms(x_ref[...], g_ref[...])
        hn_ref[...] = (y * (1.0 + sc_ref[0]) + sh_ref[0]).astype(BF16)

    acc = _dot(hn_ref[...], w_ref[...])
    o_ref[...] = acc.astype(BF16)

    @pl.when(j == nj - 1)
    def _():
        side_ref[...] = acc[:, IN_TN - SIDE_W:] + sb_ref[...]


def _in_proj(x2d, g, sc, sh, w_in_r, side_b, L):
    T = x2d.shape[0]
    tm = min(1024, L)
    nj = IN_COLS_R // IN_TN
    bidx = lambda i, j: ((i * tm) // L, 0, 0)
    return pl.pallas_call(
        functools.partial(_in_kernel, nj=nj), name="in_proj",
        grid=(T // tm, nj),
        in_specs=[
            pl.BlockSpec((tm, D_MODEL), lambda i, j: (i, 0)),
            pl.BlockSpec((1, D_MODEL), lambda i, j: (0, 0)),
            pl.BlockSpec((1, 1, D_MODEL), bidx),
            pl.BlockSpec((1, 1, D_MODEL), bidx),
            pl.BlockSpec((D_MODEL, IN_TN), lambda i, j: (0, j)),
            pl.BlockSpec((1, SIDE_W), lambda i, j: (0, 0)),
        ],
        out_specs=[
            pl.BlockSpec((tm, IN_TN), lambda i, j: (i, j)),
            pl.BlockSpec((tm, SIDE_W), lambda i, j: (i, 0)),
        ],
        out_shape=[
            jax.ShapeDtypeStruct((T, IN_COLS_R), BF16),
            jax.ShapeDtypeStruct((T, SIDE_W), F32),
        ],
        scratch_shapes=[pltpu.VMEM((tm, D_MODEL), BF16)],
        compiler_params=_cparams("arbitrary", "arbitrary"),
    )(x2d, g, sc, sh, w_in_r, side_b)


def _mlstm_chunk(q, k, v, li_col, lf_col, b_col, li_row, b_row, mask, C_ref, n_ref, m_ref, idx):
    m = m_ref[idx]
    C = C_ref[idx]
    n = n_ref[idx]
    Dm = jnp.where(mask, b_col - b_row + li_row, -jnp.inf)
    inter = b_col + m
    m_row = jnp.maximum(inter, jnp.max(Dm, axis=-1, keepdims=True))
    w_inter = jnp.exp(inter - m_row)
    s = _dot_nt(q, k) * jnp.exp(Dm - m_row)
    num = w_inter * _dot(q, C.astype(BF16)) + _dot(s.astype(BF16), v)
    qn = jnp.sum(q.astype(F32) * n, axis=-1, keepdims=True)
    den = w_inter * qn + jnp.sum(s, axis=-1, keepdims=True)
    h = num / jnp.maximum(jnp.abs(den), jnp.exp(-m_row))
    bL = jnp.sum(lf_col, axis=0, keepdims=True)
    g = bL - b_col + li_col
    m_new = jnp.maximum(bL + m, jnp.max(g, axis=0, keepdims=True))
    a = jnp.exp(bL + m - m_new)
    kw = k.astype(F32) * jnp.exp(g - m_new)
    C_ref[idx] = a * C + _dot(kw.T.astype(BF16), v)
    n_ref[idx] = a * n + jnp.sum(kw, axis=0, keepdims=True)
    m_ref[idx] = m_new
    return h


def _mlstm_kernel(qf_ref, kf_ref, vf_ref, sf_ref, qb_ref, kb_ref, vb_ref, sb_ref,
                  hf_ref, hb_ref, C_ref, n_ref, m_ref):
    c = pl.program_id(1)

    @pl.when(c == 0)
    def _():
        C_ref[...] = jnp.zeros_like(C_ref)
        n_ref[...] = jnp.zeros_like(n_ref)
        m_ref[...] = jnp.zeros_like(m_ref)

    row = lax.broadcasted_iota(I32, (ML_CHUNK, ML_CHUNK), 0)
    col = lax.broadcasted_iota(I32, (ML_CHUNK, ML_CHUNK), 1)
    lower = (col <= row)
    upper = (col >= row)
    tri_l = lower.astype(F32)
    tri_u = upper.astype(F32)
    kscale = ML_HEAD_DIM ** -0.5

    for d, (q_ref, k_ref, v_ref, s_ref, o_ref) in enumerate(
            ((qf_ref, kf_ref, vf_ref, sf_ref, hf_ref), (qb_ref, kb_ref, vb_ref, sb_ref, hb_ref))):
        gsub = s_ref[:, SIDE_GATE:SIDE_GATE + LANES]
        gT = gsub.T
        lsig = _log_sigmoid(gsub)
        lsigT = _log_sigmoid(gT)
        if d == 0:
            b_all = _dot(tri_l, lsig, precision=HI)
            b_allT = _dot(lsigT, tri_u, precision=HI)
            mask = lower
        else:
            b_all = _dot(tri_u, lsig, precision=HI)
            b_allT = _dot(lsigT, tri_l, precision=HI)
            mask = upper
        for h in range(ML_HEADS):
            ci = (2 * d) * ML_HEADS + h
            cf = (2 * d + 1) * ML_HEADS + h
            hs = slice(h * ML_HEAD_DIM, (h + 1) * ML_HEAD_DIM)
            q = q_ref[:, hs]
            k = k_ref[:, hs] * kscale
            v = v_ref[:, hs]
            hout = _mlstm_chunk(
                q, k, v,
                gsub[:, ci:ci + 1], lsig[:, cf:cf + 1], b_all[:, cf:cf + 1],
                gT[ci:ci + 1, :], b_allT[cf:cf + 1, :], mask,
                C_ref, n_ref, m_ref, d * ML_HEADS + h)
            o_ref[:, hs] = hout.astype(BF16)


def _mlstm(proj, side, B, L):
    T = B * L
    nc = L // ML_CHUNK
    fwd = lambda cb: (lambda b, c: (b * nc + c, cb))
    bwd = lambda cb: (lambda b, c: (b * nc + nc - 1 - c, cb))
    blk = lambda im: pl.BlockSpec((ML_CHUNK, ML_WIDTH), im)
    sblk = lambda im: pl.BlockSpec((ML_CHUNK, SIDE_W), im)
    nst = 2 * ML_HEADS
    return pl.pallas_call(
        _mlstm_kernel, name="mlstm",
        grid=(B, nc),
        in_specs=[blk(fwd(0)), blk(fwd(1)), blk(fwd(2)), sblk(fwd(0)),
                  blk(bwd(0)), blk(bwd(1)), blk(bwd(2)), sblk(bwd(0))],
        out_specs=[blk(fwd(0)), blk(bwd(0))],
        out_shape=[jax.ShapeDtypeStruct((T, ML_WIDTH), BF16)] * 2,
        scratch_shapes=[pltpu.VMEM((nst, ML_HEAD_DIM, ML_HEAD_DIM), F32),
                        pltpu.VMEM((nst, 1, ML_HEAD_DIM), F32),
                        pltpu.VMEM((nst, 1, 1), F32)],
        compiler_params=_cparams("arbitrary", "arbitrary"),
    )(proj, proj, proj, side, proj, proj, proj, side)


class _FftPlan:
    def __init__(self, L):
        self.L = L
        self.N = 2 * L
        self.N2 = FFT_N2
        self.N1 = self.N // self.N2
        self.N1h = self.N1 // 2
        self.K1 = self.N1h + 1
        self.K1p = -(-self.K1 // SUBLANES) * SUBLANES
        self.PT = self.N2 + FFT_TPAD
        self.P2 = 2 * self.N2 + FFT_KPAD


def _fft_tables(p):
    k1 = jnp.arange(p.K1p, dtype=I32)
    n1 = jnp.arange(p.N1h, dtype=I32)
    n2 = jnp.arange(p.N2, dtype=I32)
    n = p.N2 * n1[None, :] + n2[:, None]
    ph = (k1[None, :, None] * n[:, None, :]) % p.N
    ang = ph.astype(F32) * (2.0 * math.pi / p.N)
    valid = (k1 < p.K1)[None, :, None]
    c = jnp.where(valid, jnp.cos(ang), 0.0)
    s = jnp.where(valid, jnp.sin(ang), 0.0)
    t1 = jnp.concatenate([c, -s], axis=1).astype(BF16)
    wk = jnp.where((k1 == 0) | (k1 == p.N1h), 1.0, 2.0) / p.N
    wk = jnp.where(k1 < p.K1, wk, 0.0)[None, None, :]
    ct = jnp.swapaxes(c, 1, 2) * wk
    st = jnp.swapaxes(s, 1, 2) * wk
    t3 = jnp.concatenate([ct, -st], axis=2).astype(BF16)
    a2 = ((n2[:, None] * n2[None, :]) % p.N2).astype(F32) * (2.0 * math.pi / p.N2)
    c2, s2 = jnp.cos(a2), jnp.sin(a2)
    g2f = jnp.block([[c2, s2], [-s2, c2]]).astype(BF16)
    g2i = jnp.block([[c2, -s2], [s2, c2]]).astype(BF16)
    return t1, t3, g2f, g2i


def _pack_bf16_pair(lo, hi):
    lb = lax.bitcast_convert_type(lo.astype(BF16).astype(F32), U32)
    hb = lax.bitcast_convert_type(hi.astype(BF16).astype(F32), U32)
    return hb | (lb >> 16)


def _unpack_bf16_pair(w):
    lo = lax.bitcast_convert_type(w << 16, F32)
    hi = lax.bitcast_convert_type(w & jnp.uint32(0xFFFF0000), F32)
    return lo, hi


def _pack_lanes(x):
    return _pack_bf16_pair(x[:, :LANES], x[:, LANES:])


def _unpack_lanes(w):
    lo, hi = _unpack_bf16_pair(w)
    return jnp.concatenate([lo.astype(BF16), hi.astype(BF16)], axis=1)


def _fft_stage1(p, tst_ref, kd_ref, t1_ref):
    def body(n2, carry):
        xs = _unpack_lanes(tst_ref[pl.ds(n2, p.N1h, stride=p.PT), :])
        w = _pack_lanes(_dot(t1_ref[n2], xs))
        kd_ref[pl.ds(n2, p.K1p, stride=p.P2), :] = w[:p.K1p]
        kd_ref[pl.ds(p.N2 + n2, p.K1p, stride=p.P2), :] = w[p.K1p:]
        return carry
    lax.fori_loop(0, p.N2, body, 0, unroll=FFT_UNROLL_OUTER)


def _hyfeat_kernel(w1t_ref, w1c_ref, w1s_ref, b1_ref, w2_ref, b2_ref, fr_ref, o_ref, *, L, rows):
    i = pl.program_id(0)
    t = (lax.broadcasted_iota(I32, (rows, 1), 0) + i * rows).astype(F32) / L
    bands = (lax.broadcasted_iota(I32, (1, LANES), 1) + 1).astype(F32)
    ang = (2.0 * math.pi * t) * bands
    z = (t * w1t_ref[...] + _dot(jnp.cos(ang), w1c_ref[...], precision=HI)
         + _dot(jnp.sin(ang), w1s_ref[...], precision=HI) + b1_ref[...])
    h = jnp.sin(fr_ref[0:1, :] * z)
    h = jnp.sin(fr_ref[1:2, :] * (_dot(h, w2_ref[...], precision=HI) + b2_ref[...]))
    o_ref[...] = h


def _hyena_features(L, fw1, fb1, fw2, fb2, freq):
    rows = min(512, L)
    w1c = jnp.zeros((LANES, HY_FFN), F32).at[:HY_BANDS].set(fw1[1:1 + HY_BANDS])
    w1s = jnp.zeros((LANES, HY_FFN), F32).at[:HY_BANDS].set(fw1[1 + HY_BANDS:])
    full = lambda shp: pl.BlockSpec(shp, lambda i: (0,) * len(shp))
    return pl.pallas_call(
        functools.partial(_hyfeat_kernel, L=L, rows=rows), name="hy_feat",
        grid=(L // rows,),
        in_specs=[full((1, HY_FFN)), full((LANES, HY_FFN)), full((LANES, HY_FFN)), full((1, HY_FFN)),
                  full((HY_FFN, HY_FFN)), full((1, HY_FFN)), full((2, HY_FFN))],
        out_specs=pl.BlockSpec((rows, HY_FFN), lambda i: (i, 0)),
        out_shape=jax.ShapeDtypeStruct((L, HY_FFN), F32),
        compiler_params=_cparams("arbitrary"),
    )(fw1[0:1], w1c, w1s, fb1.reshape(1, -1), fw2, fb2.reshape(1, -1), freq)


def _hyfilt_kernel(h2_ref, wf_ref, wb_ref, df_ref, db_ref, skip_ref, t1_ref, g2_ref, o_ref,
                   tst_ref, kd_ref, *, p):
    L, N2 = p.L, p.N2
    decay_f = jnp.exp(df_ref[...])
    decay_b = jnp.exp(db_ref[...])
    w_fb = jnp.concatenate([wf_ref[...], wb_ref[...]], axis=1)

    def gen(n1, carry):
        r0 = pl.multiple_of(n1 * N2, N2)
        t = (lax.broadcasted_iota(I32, (N2, 1), 0) + r0).astype(F32) / L
        h2 = h2_ref[pl.ds(r0, N2), :]
        fb = _dot(h2, w_fb, precision=HI)
        f = fb[:, :LANES] * jnp.exp(-t * decay_f)
        b = fb[:, LANES:] * jnp.exp(-t * decay_b)
        b = jnp.where(t == 0.0, 0.0, b)
        s0 = pl.multiple_of(n1 * p.PT, SUBLANES)
        tst_ref[pl.ds(s0, N2), :] = _pack_bf16_pair(f, b)
        return carry + jnp.sum(jnp.abs(f) + jnp.abs(b), axis=0, keepdims=True)

    l1 = lax.fori_loop(0, p.N1h, gen, jnp.zeros((1, LANES), F32), unroll=2)
    inv = 1.0 / l1
    _fft_stage1(p, tst_ref, kd_ref, t1_ref)

    def spec(k1, carry):
        s0 = pl.multiple_of(k1 * p.P2, SUBLANES)
        x = _dot(g2_ref[...], _unpack_lanes(kd_ref[pl.ds(s0, 2 * N2), :]))
        xf, xb = x[:, :LANES], x[:, LANES:]
        kr = (xf[:N2] + xb[:N2]) * inv + skip_ref[...]
        ki = (xf[N2:] - xb[N2:]) * inv
        o_ref[k1] = jnp.concatenate([kr, ki], axis=0).astype(BF16)
        return carry
    lax.fori_loop(0, p.K1, spec, 0, unroll=FFT_UNROLL_MID)


def _hyena_spectra(p, h2, fw3, log_decay, skip, t1, g2f):
    nct = HY_WIDTH // LANES
    col = lambda d: (lambda o, c: (0, (2 * o + d) * nct + c))
    full = lambda shp: pl.BlockSpec(shp, lambda o, c: (0,) * len(shp))
    ld = log_decay.reshape(1, -1)
    return pl.pallas_call(
        functools.partial(_hyfilt_kernel, p=p), name="hy_spectra",
        grid=(HY_ORDER, nct),
        in_specs=[full((p.L, HY_FFN)),
                  pl.BlockSpec((HY_FFN, LANES), col(0)), pl.BlockSpec((HY_FFN, LANES), col(1)),
                  pl.BlockSpec((1, LANES), col(0)), pl.BlockSpec((1, LANES), col(1)),
                  pl.BlockSpec((None, 1, LANES), lambda o, c: (o, 0, c)),
                  full(t1.shape), full(g2f.shape)],
        out_specs=pl.BlockSpec((None, p.K1, 2 * p.N2, LANES), lambda o, c: (o, 0, 0, c)),
        out_shape=jax.ShapeDtypeStruct((HY_ORDER, p.K1, 2 * p.N2, HY_WIDTH), BF16),
        scratch_shapes=[pltpu.VMEM((p.N1h * p.PT, LANES), U32), pltpu.VMEM((p.K1p * p.P2, LANES), U32)],
        compiler_params=_cparams("arbitrary", "arbitrary"),
    )(h2, fw3, fw3, ld, ld, skip.reshape(HY_ORDER, 1, HY_WIDTH), t1, g2f)


def _conv3_chunk(u_ref, cw_ref, n1, nchunks, L):
    N2 = FFT_N2
    lanes = u_ref.shape[1]
    pk = 2 * SUBLANES
    r0 = pl.multiple_of(n1 * N2, N2)
    cur = u_ref[pl.ds(r0, N2), :].astype(F32)
    pstart = pl.multiple_of(jnp.maximum(r0 - pk, 0), pk)
    nstart = pl.multiple_of(jnp.minimum(r0 + N2, L - pk), pk)
    prev = u_ref[pl.ds(pstart, pk), :].astype(F32)[pk - 1:pk, :]
    nxt = u_ref[pl.ds(nstart, pk), :].astype(F32)[0:1, :]
    prev = prev * jnp.where(n1 > 0, 1.0, 0.0)
    nxt = nxt * jnp.where(n1 < nchunks - 1, 1.0, 0.0)
    row = lax.broadcasted_iota(I32, (N2, lanes), 0)
    up = jnp.where(row == 0, prev, pltpu.roll(cur, 1, axis=0))
    dn = jnp.where(row == N2 - 1, nxt, pltpu.roll(cur, N2 - 1, axis=0))
    return cw_ref[0:1, :] * up + cw_ref[1:2, :] * cur + cw_ref[2:3, :] * dn + cw_ref[3:4, :]


def _hyconv_kernel(ua_ref, ub_ref, cwa_ref, cwb_ref, sp_ref, t1_ref, t3_ref, g2f_ref, g2i_ref,
                   o_ref, tst_ref, kd_ref, *, p, pre_a):
    L, N2 = p.L, p.N2

    def chunk_a(n1):
        if pre_a:
            return _conv3_chunk(ua_ref, cwa_ref, n1, p.N1h, L)
        return ua_ref[pl.ds(pl.multiple_of(n1 * N2, N2), N2), :].astype(F32)

    def load(n1, carry):
        tst_ref[pl.ds(pl.multiple_of(n1 * p.PT, SUBLANES), N2), :] = _pack_lanes(chunk_a(n1))
        return carry
    lax.fori_loop(0, p.N1h, load, 0, unroll=2)

    _fft_stage1(p, tst_ref, kd_ref, t1_ref)

    def mid(k1, carry):
        s0 = pl.multiple_of(k1 * p.P2, SUBLANES)
        x = _dot(g2f_ref[...], _unpack_lanes(kd_ref[pl.ds(s0, 2 * N2), :]))
        xr, xi = x[:N2], x[N2:]
        kk = sp_ref[k1].astype(F32)
        kr, ki = kk[:N2], kk[N2:]
        y = jnp.concatenate([xr * kr - xi * ki, xr * ki + xi * kr], axis=0).astype(BF16)
        kd_ref[pl.ds(s0, 2 * N2), :] = _pack_lanes(_dot(g2i_ref[...], y))
        return carry
    lax.fori_loop(0, p.K1, mid, 0, unroll=FFT_UNROLL_MID)

    def stage3(n2, carry):
        br = kd_ref[pl.ds(n2, p.K1p, stride=p.P2), :]
        bi = kd_ref[pl.ds(N2 + n2, p.K1p, stride=p.P2), :]
        bc = _unpack_lanes(jnp.concatenate([br, bi], axis=0))
        tst_ref[pl.ds(n2, p.N1h, stride=p.PT), :] = _pack_lanes(_dot(t3_ref[n2], bc))
        return carry
    lax.fori_loop(0, N2, stage3, 0, unroll=FFT_UNROLL_OUTER)

    def store(n1, carry):
        lo, hi = _unpack_bf16_pair(tst_ref[pl.ds(pl.multiple_of(n1 * p.PT, SUBLANES), N2), :])
        y = jnp.concatenate([lo, hi], axis=1)
        g = _conv3_chunk(ub_ref, cwb_ref, n1, p.N1h, L)
        o_ref[pl.ds(pl.multiple_of(n1 * N2, N2), N2), :] = (g * y).astype(BF16)
        return carry
    lax.fori_loop(0, p.N1h, store, 0, unroll=2)


def _hyconv(p, a_arr, a_col, b_arr, b_col, cw_pack, spectra, order, tables, B, pre_a):
    L = p.L
    T = B * L
    nct = HY_WIDTH // HY_CT
    t1, t3, g2f, g2i = tables
    seq_bytes = L * HY_CT * 2
    const_bytes = 2 * (t1.size + t3.size + p.K1 * 2 * p.N2 * HY_CT)
    scratch_bytes = 4 * LANES * (p.N1h * p.PT + p.K1p * p.P2)
    single = 6 * seq_bytes + 2 * const_bytes + scratch_bytes > VMEM_LIMIT
    mode = dict(pipeline_mode=pl.Buffered(1)) if single else {}
    full = lambda shp: pl.BlockSpec(shp, lambda c, b: (0,) * len(shp), **mode)
    acb, bcb = a_col // HY_CT, b_col // HY_CT
    cwa_cb = (a_col - COL_HY) // HY_CT if pre_a else 0
    cwb_cb = (b_col - COL_HY) // HY_CT
    return pl.pallas_call(
        functools.partial(_hyconv_kernel, p=p, pre_a=pre_a), name="hy_conv",
        grid=(nct, B),
        in_specs=[pl.BlockSpec((L, HY_CT), lambda c, b: (b, acb + c), **mode),
                  pl.BlockSpec((L, HY_CT), lambda c, b: (b, bcb + c), **mode),
                  pl.BlockSpec((SUBLANES, HY_CT), lambda c, b: (0, cwa_cb + c)),
                  pl.BlockSpec((SUBLANES, HY_CT), lambda c, b: (0, cwb_cb + c)),
                  pl.BlockSpec((None, p.K1, 2 * p.N2, HY_CT), lambda c, b: (order, 0, 0, c), **mode),
                  full(t1.shape), full(t3.shape), full(g2f.shape), full(g2i.shape)],
        out_specs=pl.BlockSpec((L, HY_CT), lambda c, b: (b, c)),
        out_shape=jax.ShapeDtypeStruct((T, HY_WIDTH), BF16),
        scratch_shapes=[pltpu.VMEM((p.N1h * p.PT, LANES), U32), pltpu.VMEM((p.K1p * p.P2, LANES), U32)],
        compiler_params=_cparams("arbitrary", "arbitrary"),
    )(a_arr, b_arr, cw_pack, cw_pack, spectra, t1, t3, g2f, g2i)


def _mlaprep_kernel(cq_ref, ckv_ref, side_ref, cs_ref, gq_ref, gkv_ref, wuq_ref, wuk_ref, wuvt_ref,
                    q_ref, k_ref, vt_ref):
    cqn = _rms(cq_ref[...].astype(F32), gq_ref[...]).astype(BF16)
    ckvn = _rms(ckv_ref[...].astype(F32), gkv_ref[...]).astype(BF16)
    qa = _dot(cqn, wuq_ref[...])
    kn = _dot(ckvn, wuk_ref[...])
    vt = _dot_nt(wuvt_ref[...], ckvn)
    cs = cs_ref[...]
    cos2, sin2 = cs[:, :MLA_ROPE], cs[:, MLA_ROPE:]
    side = side_ref[...]
    k_rope = (side[:, :MLA_ROPE] * cos2 + side[:, MLA_ROPE:2 * MLA_ROPE] * sin2).astype(BF16)
    scale = MLA_QK ** -0.5 * math.log2(math.e)
    hw = MLA_NOPE + 2 * MLA_ROPE
    extra = MLA_VE - MLA_V
    ones_rows = jnp.where(lax.broadcasted_iota(I32, (extra, vt.shape[1]), 0) == 0, 1.0, 0.0).astype(BF16)
    for h in range(MLA_HEADS):
        b0 = h * hw
        q_rope = qa[:, b0 + MLA_NOPE:b0 + MLA_QK] * cos2 + qa[:, b0 + MLA_QK:b0 + hw] * sin2
        q_ref[h, :, :MLA_NOPE] = (qa[:, b0:b0 + MLA_NOPE] * scale).astype(BF16)
        q_ref[h, :, MLA_NOPE:] = (q_rope * scale).astype(BF16)
        k_ref[h, :, :MLA_NOPE] = kn[:, h * MLA_NOPE:(h + 1) * MLA_NOPE].astype(BF16)
        k_ref[h, :, MLA_NOPE:] = k_rope
        vt_ref[h, :MLA_V, :] = vt[h * MLA_V:(h + 1) * MLA_V, :].astype(BF16)
        vt_ref[h, MLA_V:, :] = ones_rows


def _mla_prep(proj, side, cs, gq, gkv, wuq_ext, wuk, wuvt, B, L):
    T = B * L
    tm = min(512, L)
    nb = L // tm
    hw = MLA_NOPE + 2 * MLA_ROPE
    full = lambda shp: pl.BlockSpec(shp, lambda i: (0,) * len(shp))
    oidx = lambda i: (i // nb, 0, i % nb, 0)
    return pl.pallas_call(
        _mlaprep_kernel, name="mla_prep",
        grid=(T // tm,),
        in_specs=[pl.BlockSpec((tm, MLA_Q_RANK), lambda i: (i, COL_CQ // MLA_Q_RANK)),
                  pl.BlockSpec((tm, MLA_KV_RANK), lambda i: (i, COL_CKV // MLA_KV_RANK)),
                  pl.BlockSpec((tm, SIDE_W), lambda i: (i, 0)),
                  pl.BlockSpec((tm, 2 * MLA_ROPE), lambda i: (i % nb, 0)),
                  full((1, MLA_Q_RANK)), full((1, MLA_KV_RANK)),
                  full((MLA_Q_RANK, MLA_HEADS * hw)), full((MLA_KV_RANK, MLA_HEADS * MLA_NOPE)),
                  full((MLA_HEADS * MLA_V, MLA_KV_RANK))],
        out_specs=[pl.BlockSpec((None, MLA_HEADS, tm, MLA_QK), oidx),
                   pl.BlockSpec((None, MLA_HEADS, tm, MLA_QK), oidx),
                   pl.BlockSpec((None, MLA_HEADS, MLA_VE, tm), lambda i: (i // nb, 0, 0, i % nb))],
        out_shape=[jax.ShapeDtypeStruct((B, MLA_HEADS, L, MLA_QK), BF16),
                   jax.ShapeDtypeStruct((B, MLA_HEADS, L, MLA_QK), BF16),
                   jax.ShapeDtypeStruct((B, MLA_HEADS, MLA_VE, L), BF16)],
        compiler_params=_cparams("arbitrary"),
    )(proj, proj, side, cs, gq, gkv, wuq_ext, wuk, wuvt)


def _flash_kernel(q_ref, k_ref, vt_ref, o_ref, *, tq, tk, nsplit):
    L = k_ref.shape[0]
    sub = tq // nsplit
    qs = [q_ref[pl.ds(i * sub, sub), :] for i in range(nsplit)]

    units = [(j, i) for j in range(L // tk) for i in range(nsplit)]

    def scores(unit):
        j, i = unit
        return _dot_nt(k_ref[pl.ds(j * tk, tk), :], qs[i])

    m = [jnp.full((1, sub), -jnp.inf, F32) for _ in range(nsplit)]
    acc = [jnp.zeros((MLA_VE, sub), F32) for _ in range(nsplit)]
    st = scores(units[0])
    for n, (j, i) in enumerate(units):
        st_next = scores(units[n + 1]) if n + 1 < len(units) else None
        m_new = jnp.maximum(m[i], jnp.max(st, axis=0, keepdims=True))
        alpha = jnp.exp2(m[i] - m_new)
        pt = jnp.exp2(st - m_new).astype(BF16)
        acc[i] = alpha * acc[i] + _dot(vt_ref[:, pl.ds(j * tk, tk)], pt)
        m[i] = m_new
        st = st_next
    for i in range(nsplit):
        o_ref[pl.ds(i * sub, sub), :] = (acc[i][:MLA_V] / acc[i][MLA_V:MLA_V + 1]).T.astype(BF16)


def _flash(q, k, v, B, L):
    tq = min(ATTN_TQ, L)
    tk = min(ATTN_TK, L)
    nq = L // tq
    return pl.pallas_call(
        functools.partial(_flash_kernel, tq=tq, tk=tk, nsplit=ATTN_SPLIT), name="mla_attn",
        grid=(B, MLA_HEADS, nq),
        in_specs=[pl.BlockSpec((None, None, tq, MLA_QK), lambda b, h, i: (b, h, i, 0)),
                  pl.BlockSpec((None, None, L, MLA_QK), lambda b, h, i: (b, h, 0, 0)),
                  pl.BlockSpec((None, None, MLA_VE, L), lambda b, h, i: (b, h, 0, 0))],
        out_specs=pl.BlockSpec((tq, MLA_V), lambda b, h, i: (b * nq + i, h)),
        out_shape=jax.ShapeDtypeStruct((B * L, MLA_WIDTH), BF16),
        compiler_params=_cparams("arbitrary", "arbitrary", "arbitrary"),
    )(q, k, v)


def _merge_kernel(hf_ref, hb_ref, mlo_ref, mlg_ref, ohy_ref, omla_ref, gml_ref, ghy_ref, gmla_ref,
                  wml_ref, why_ref, wmla_ref, o_ref, oml_ref):
    @pl.when(pl.program_id(1) == 0)
    def _():
        for h in range(ML_HEADS):
            hs = slice(h * ML_HEAD_DIM, (h + 1) * ML_HEAD_DIM)
            x = hf_ref[:, hs].astype(F32) + hb_ref[:, hs].astype(F32)
            y = _rms(x, mlg_ref[:, hs])
            oml_ref[:, hs] = (y * _sigmoid(mlo_ref[:, hs].astype(F32))).astype(BF16)

    acc = _sigmoid(gml_ref[...].astype(F32)) * _dot(oml_ref[...], wml_ref[...])
    acc += _sigmoid(ghy_ref[...].astype(F32)) * _dot(ohy_ref[...], why_ref[...])
    acc += _sigmoid(gmla_ref[...].astype(F32)) * _dot(omla_ref[...], wmla_ref[...])
    o_ref[...] = acc.astype(BF16)


def _merge(hf, hb, proj, ml_norm_g, o_hy, o_mla, wml, why, wmla, L):
    T = hf.shape[0]
    tm = min(1024, L)
    tn = 512
    row = lambda cb: (lambda i, j: (i, cb))
    gate = lambda k: (lambda i, j: (i, (COL_BRG + k * D_MODEL) // tn + j))
    wsp = pl.BlockSpec((ML_WIDTH, tn), lambda i, j: (0, j))
    return pl.pallas_call(
        _merge_kernel, name="merge",
        grid=(T // tm, D_MODEL // tn),
        in_specs=[pl.BlockSpec((tm, ML_WIDTH), row(0)), pl.BlockSpec((tm, ML_WIDTH), row(0)),
                  pl.BlockSpec((tm, ML_WIDTH), row(3)),
                  pl.BlockSpec((1, ML_WIDTH), lambda i, j: (0, 0)),
                  pl.BlockSpec((tm, HY_WIDTH), row(0)), pl.BlockSpec((tm, MLA_WIDTH), row(0)),
                  pl.BlockSpec((tm, tn), gate(0)), pl.BlockSpec((tm, tn), gate(1)), pl.BlockSpec((tm, tn), gate(2)),
                  wsp, wsp, wsp],
        out_specs=pl.BlockSpec((tm, tn), lambda i, j: (i, j)),
        out_shape=jax.ShapeDtypeStruct((T, D_MODEL), BF16),
        scratch_shapes=[pltpu.VMEM((tm, ML_WIDTH), BF16)],
        compiler_params=_cparams("arbitrary", "arbitrary"),
    )(hf, hb, proj, ml_norm_g, o_hy, o_mla, proj, proj, proj, wml, why, wmla)


def _outproj_kernel(m_ref, w_ref, x_ref, g_ref, o_ref):
    o_ref[...] = x_ref[...] + g_ref[0] * _dot(m_ref[...], w_ref[...])


def _out_proj(merged, w_out, x2d, g1, L):
    T = x2d.shape[0]
    tm = min(1024, L)
    tn = 512
    return pl.pallas_call(
        _outproj_kernel, name="out_proj",
        grid=(T // tm, D_MODEL // tn),
        in_specs=[pl.BlockSpec((tm, D_MODEL), lambda i, j: (i, 0)),
                  pl.BlockSpec((D_MODEL, tn), lambda i, j: (0, j)),
                  pl.BlockSpec((tm, tn), lambda i, j: (i, j)),
                  pl.BlockSpec((1, 1, tn), lambda i, j: ((i * tm) // L, 0, j))],
        out_specs=pl.BlockSpec((tm, tn), lambda i, j: (i, j)),
        out_shape=jax.ShapeDtypeStruct((T, D_MODEL), F32),
        compiler_params=_cparams("arbitrary", "arbitrary"),
    )(merged, w_out, x2d, g1)


def _route_kernel(x_ref, g_ref, sc_ref, sh_ref, wr_ref, rb_ref, hp_ref, idx_ref, wgt_ref, lg_ref):
    half = D_MODEL // 2
    nsb = x_ref.shape[0] // LANES
    hn = _rms(x_ref[...], g_ref[...]) * (1.0 + sc_ref[0]) + sh_ref[0]
    hp_ref[...] = _pack_bf16_pair(hn[:, :half], hn[:, half:])
    for sb in range(nsb):
        logits = _dot_nt(wr_ref[...], hn[sb * LANES:(sb + 1) * LANES, :], precision=HI)
        lg_ref[pl.ds(sb * N_EXPERTS, N_EXPERTS), :] = _sigmoid(logits)
    srow = [lg_ref[pl.ds(e, nsb, stride=N_EXPERTS), :] for e in range(N_EXPERTS)]
    rows = [srow[e] + rb_ref[e:e + 1, :] for e in range(N_EXPERTS)]
    gs = []
    for g in range(N_GROUPS):
        r = rows[g * EXPERTS_PER_GROUP:(g + 1) * EXPERTS_PER_GROUP]
        best = None
        for a in range(EXPERTS_PER_GROUP):
            for b in range(a + 1, EXPERTS_PER_GROUP):
                s = r[a] + r[b]
                best = s if best is None else jnp.maximum(best, s)
        gs.append(best)
    gsel = jnp.zeros_like(gs[0], dtype=I32)
    gbest = gs[0]
    for g in range(1, N_GROUPS):
        better = gs[g] > gbest
        gsel = jnp.where(better, g, gsel)
        gbest = jnp.where(better, gs[g], gbest)
    masked = [jnp.where(gsel == (e // EXPERTS_PER_GROUP), rows[e], -jnp.inf) for e in range(N_EXPERTS)]
    picks = []
    for _ in range(2):
        bi = jnp.zeros_like(gsel)
        bv = masked[0]
        bs = srow[0]
        for e in range(1, N_EXPERTS):
            better = masked[e] > bv
            bi = jnp.where(better, e, bi)
            bv = jnp.where(better, masked[e], bv)
            bs = jnp.where(better, srow[e], bs)
        picks.append((bi, bs))
        masked = [jnp.where(bi == e, -jnp.inf, masked[e]) for e in range(N_EXPERTS)]
    (i1, s1), (i2, s2) = picks
    tot = s1 + s2
    idx_ref[0] = i1
    idx_ref[1] = i2
    wgt_ref[0] = s1 / tot
    wgt_ref[1] = s2 / tot


def _route(x2d, g, sc, sh, w_router, router_bias, L):
    T = x2d.shape[0]
    tm = min(ROUTE_TM, L)
    nsb = tm // LANES
    bidx = lambda i: ((i * tm) // L, 0, 0)
    rb = jnp.broadcast_to(router_bias.reshape(N_EXPERTS, 1), (N_EXPERTS, LANES))
    hp, idx, wgt = _route_call(x2d, g, sc, sh, w_router, rb, T, tm, nsb, bidx)
    return hp, idx.reshape(2, T), wgt.reshape(2, T)


def _route_call(x2d, g, sc, sh, w_router, rb, T, tm, nsb, bidx):
    return pl.pallas_call(
        _route_kernel, name="route",
        grid=(T // tm,),
        in_specs=[pl.BlockSpec((tm, D_MODEL), lambda i: (i, 0)),
                  pl.BlockSpec((1, D_MODEL), lambda i: (0, 0)),
                  pl.BlockSpec((1, 1, D_MODEL), bidx), pl.BlockSpec((1, 1, D_MODEL), bidx),
                  pl.BlockSpec((N_EXPERTS, D_MODEL), lambda i: (0, 0)),
                  pl.BlockSpec((N_EXPERTS, LANES), lambda i: (0, 0))],
        out_specs=[pl.BlockSpec((tm, D_MODEL // 2), lambda i: (i, 0)),
                   pl.BlockSpec((2, nsb, LANES), lambda i: (0, i, 0)),
                   pl.BlockSpec((2, nsb, LANES), lambda i: (0, i, 0))],
        out_shape=[jax.ShapeDtypeStruct((T, D_MODEL // 2), U32),
                   jax.ShapeDtypeStruct((2, T // LANES, LANES), I32),
                   jax.ShapeDtypeStruct((2, T // LANES, LANES), F32)],
        scratch_shapes=[pltpu.VMEM((nsb * N_EXPERTS, LANES), F32)],
        compiler_params=_cparams("arbitrary"),
    )(x2d, g, sc, sh, w_router.T, rb)


def _moe_plan(idx, T):
    A = 2 * T
    e = idx.reshape(A)
    tok = jnp.tile(jnp.arange(T, dtype=I32), 2)
    onehot = (e[:, None] == jnp.arange(N_EXPERTS, dtype=I32)[None, :]).astype(I32)
    csum = jnp.cumsum(onehot, axis=0)
    counts = csum[-1]
    rank = jnp.sum(csum * onehot, axis=1) - 1
    padded = ((counts + MOE_TILE - 1) // MOE_TILE) * MOE_TILE
    ends = jnp.cumsum(padded)
    starts = ends - padded
    pos = jnp.sum(starts[None, :] * onehot, axis=1) + rank
    R = A + N_EXPERTS * MOE_TILE
    src = jnp.zeros((R,), I32).at[pos].set(tok, unique_indices=True)
    n_tiles = R // MOE_TILE
    tile_start = jnp.arange(n_tiles, dtype=I32) * MOE_TILE
    tile_e = jnp.minimum(jnp.sum((tile_start[:, None] >= ends[None, :]).astype(I32), axis=1), N_EXPERTS - 1)
    num_used = (ends[-1] // MOE_TILE).astype(I32).reshape(1)
    return src, pos.reshape(2, T).astype(I32), tile_e.astype(I32), num_used


def _row_gather_copy(src_hbm, dst_ref, sem, src_row, dst_row):
    return pltpu.make_async_copy(src_hbm.at[pl.ds(src_row, 1)], dst_ref.at[pl.ds(dst_row, 1)], sem)


def _issue_row_gathers(src_hbm, idx_ref, dst_ref, sem, rows, dst_base=0):
    for r in range(rows):
        _row_gather_copy(src_hbm, dst_ref, sem, idx_ref[0, 0, r], dst_base + r).start()


def _wait_row_gathers(src_hbm, dst_ref, sem):
    pltpu.make_async_copy(src_hbm.at[pl.ds(0, dst_ref.shape[0])], dst_ref, sem).wait()


def _expert_kernel(te_ref, nt_ref, idx_ref, idxn_ref, hp_hbm, wgu_ref, wd_ref, o_ref, xbuf_ref, sems):
    t = pl.program_id(0)
    nt = nt_ref[0]
    slot = lax.rem(t, 2)
    half = D_MODEL // 2

    @pl.when(t == 0)
    def _():
        _issue_row_gathers(hp_hbm, idx_ref, xbuf_ref.at[0], sems.at[0], MOE_TILE)

    @pl.when(t < nt)
    def _():
        _wait_row_gathers(hp_hbm, xbuf_ref.at[slot], sems.at[slot])
        _issue_row_gathers(hp_hbm, idxn_ref, xbuf_ref.at[1 - slot], sems.at[1 - slot], MOE_TILE)
        lo, hi = _unpack_bf16_pair(xbuf_ref[slot])
        h1 = _dot(lo.astype(BF16), wgu_ref[:half, :]) + _dot(hi.astype(BF16), wgu_ref[half:, :])
        a, b = h1[:, :D_EXPERT], h1[:, D_EXPERT:]
        act = (a * _sigmoid(a) * b).astype(BF16)
        y = _dot(act, wd_ref[...])
        o_ref[...] = _pack_bf16_pair(y[:, :half], y[:, half:])

    @pl.when(t == nt - 1)
    def _():
        _wait_row_gathers(hp_hbm, xbuf_ref.at[1 - slot], sems.at[1 - slot])

    @pl.when(t >= nt)
    def _():
        o_ref[...] = jnp.zeros_like(o_ref)


def _experts(hp, src, tile_e, num_used, wgu, wd, l):
    R = src.shape[0]
    nt = R // MOE_TILE
    half = D_MODEL // 2
    src3 = src.reshape(nt, 1, MOE_TILE)
    grid_spec = pltpu.PrefetchScalarGridSpec(
        num_scalar_prefetch=2,
        grid=(nt,),
        in_specs=[pl.BlockSpec((1, 1, MOE_TILE), lambda t, te, n: (t, 0, 0), memory_space=pltpu.SMEM),
                  pl.BlockSpec((1, 1, MOE_TILE), lambda t, te, n: (jnp.minimum(t + 1, n[0] - 1), 0, 0),
                               memory_space=pltpu.SMEM),
                  pl.BlockSpec(memory_space=pl.ANY),
                  pl.BlockSpec((None, None, D_MODEL, 2 * D_EXPERT), lambda t, te, n: (l, te[t], 0, 0)),
                  pl.BlockSpec((None, None, D_EXPERT, D_MODEL), lambda t, te, n: (l, te[t], 0, 0))],
        out_specs=pl.BlockSpec((MOE_TILE, half), lambda t, te, n: (t, 0)),
        scratch_shapes=[pltpu.VMEM((2, MOE_TILE, half), U32), pltpu.SemaphoreType.DMA((2,))],
    )
    return pl.pallas_call(
        _expert_kernel, name="moe_experts",
        grid_spec=grid_spec,
        out_shape=jax.ShapeDtypeStruct((R, half), U32),
        compiler_params=_cparams("arbitrary"),
    )(tile_e, num_used, src3, src3, hp, wgu, wd)


def _combine_kernel(p1_ref, p2_ref, p1n_ref, p2n_ref, ys_hbm, w_ref, x_ref, g_ref, fg_ref, o_ref,
                    buf0_ref, buf1_ref, sems, *, rows, final):
    i = pl.program_id(0)
    n = pl.num_programs(0)
    half = D_MODEL // 2

    @pl.when(i == 0)
    def _():
        _issue_row_gathers(ys_hbm, p1_ref, buf0_ref, sems.at[0], rows)
        _issue_row_gathers(ys_hbm, p2_ref, buf0_ref, sems.at[0], rows, rows)

    def run(cur_ref, nxt_ref, cur_sem, nxt_sem):
        _wait_row_gathers(ys_hbm, cur_ref, cur_sem)
        _issue_row_gathers(ys_hbm, p1n_ref, nxt_ref, nxt_sem, rows)
        _issue_row_gathers(ys_hbm, p2n_ref, nxt_ref, nxt_sem, rows, rows)

        w = w_ref[...]
        w1, w2 = w[:, 0:1], w[:, 1:2]
        lo1, hi1 = _unpack_bf16_pair(cur_ref[pl.ds(0, rows), :])
        lo2, hi2 = _unpack_bf16_pair(cur_ref[pl.ds(rows, rows), :])
        g = g_ref[0]
        xlo = x_ref[:, :half] + g[:, :half] * (w1 * lo1 + w2 * lo2)
        xhi = x_ref[:, half:] + g[:, half:] * (w1 * hi1 + w2 * hi2)
        if final:
            ms = (jnp.sum(xlo * xlo, axis=-1, keepdims=True)
                  + jnp.sum(xhi * xhi, axis=-1, keepdims=True)) / D_MODEL
            r = lax.rsqrt(ms + EPS)
            xlo = xlo * r * fg_ref[:, :half]
            xhi = xhi * r * fg_ref[:, half:]
        o_ref[:, :half] = xlo
        o_ref[:, half:] = xhi

        @pl.when(i == n - 1)
        def _():
            _wait_row_gathers(ys_hbm, nxt_ref, nxt_sem)

    even = lax.rem(i, 2) == 0

    @pl.when(even)
    def _():
        run(buf0_ref, buf1_ref, sems.at[0], sems.at[1])

    @pl.when(jnp.logical_not(even))
    def _():
        run(buf1_ref, buf0_ref, sems.at[1], sems.at[0])


def _combine(ys, pos, wgt, x2d, g2, final_g, L, final):
    T = x2d.shape[0]
    rows = GATHER_TILE
    nt = T // rows
    half = D_MODEL // 2
    cur = lambda: pl.BlockSpec((1, 1, rows), lambda i: (i, 0, 0), memory_space=pltpu.SMEM)
    nxt = lambda: pl.BlockSpec((1, 1, rows), lambda i: (jnp.minimum(i + 1, nt - 1), 0, 0), memory_space=pltpu.SMEM)
    p1 = pos[0].reshape(nt, 1, rows)
    p2 = pos[1].reshape(nt, 1, rows)
    return pl.pallas_call(
        functools.partial(_combine_kernel, rows=rows, final=final), name="moe_combine",
        grid=(nt,),
        in_specs=[cur(), cur(), nxt(), nxt(),
                  pl.BlockSpec(memory_space=pl.ANY),
                  pl.BlockSpec((rows, 2), lambda i: (i, 0)),
                  pl.BlockSpec((rows, D_MODEL), lambda i: (i, 0)),
                  pl.BlockSpec((1, 1, D_MODEL), lambda i: ((i * rows) // L, 0, 0)),
                  pl.BlockSpec((1, D_MODEL), lambda i: (0, 0))],
        out_specs=pl.BlockSpec((rows, D_MODEL), lambda i: (i, 0)),
        out_shape=jax.ShapeDtypeStruct((T, D_MODEL), F32),
        scratch_shapes=[pltpu.VMEM((2 * rows, half), U32), pltpu.VMEM((2 * rows, half), U32),
                        pltpu.SemaphoreType.DMA((2,))],
        compiler_params=_cparams("arbitrary"),
    )(p1, p2, p1, p2, ys, wgt.T, x2d, g2, final_g)


def _swap_halves(w):
    h = w.shape[-1] // 2
    return jnp.concatenate([w[..., h:], w[..., :h]], axis=-1)


def _prep_layer(P, l):
    w = P["w_in"][l]
    o_g = 4 * ML_WIDTH
    o_hy = o_g + 4 * ML_HEADS
    o_cq = o_hy + 3 * HY_WIDTH
    o_ckv = o_cq + MLA_Q_RANK
    o_kr = o_ckv + MLA_KV_RANK
    o_br = o_kr + MLA_ROPE
    kr = w[:, o_kr:o_br]
    side_pad = jnp.zeros((D_MODEL, SIDE_W - SIDE_GATE - 4 * ML_HEADS), F32)
    w_in_r = jnp.concatenate(
        [w[:, :o_g], w[:, o_hy:o_cq], w[:, o_br:], w[:, o_cq:o_ckv], w[:, o_ckv:o_kr],
         kr, _swap_halves(kr), w[:, o_g:o_hy], side_pad], axis=1).astype(BF16)
    side_b = jnp.zeros((1, SIDE_W), F32).at[0, SIDE_GATE:SIDE_GATE + 4 * ML_HEADS].set(P["ml_gate_b"][l].reshape(-1))
    wuq = P["w_uq"][l].reshape(MLA_Q_RANK, MLA_HEADS, MLA_QK)
    qr = wuq[..., MLA_NOPE:]
    wuq_ext = jnp.concatenate([wuq, _swap_halves(qr)], axis=-1).reshape(MLA_Q_RANK, -1).astype(BF16)
    wukv = P["w_ukv"][l].reshape(MLA_KV_RANK, MLA_HEADS, MLA_NOPE + MLA_V)
    cw = jnp.concatenate([P["hy_conv_w"][l], P["hy_conv_b"][l][None, :],
                          jnp.zeros((SUBLANES - 4, 3 * HY_WIDTH), F32)], axis=0)
    return dict(
        w_in_r=w_in_r, side_b=side_b, wuq_ext=wuq_ext, cw=cw,
        wuk=wukv[..., :MLA_NOPE].reshape(MLA_KV_RANK, -1).astype(BF16),
        wuvt=wukv[..., MLA_NOPE:].reshape(MLA_KV_RANK, -1).T.astype(BF16),
        wml=P["w_br_ml"][l].astype(BF16), why=P["w_br_hy"][l].astype(BF16), wmla=P["w_br_mla"][l].astype(BF16),
        w_out=P["w_out"][l].astype(BF16),
        norm1_g=P["norm1_g"][l].reshape(1, -1), norm2_g=P["norm2_g"][l].reshape(1, -1),
        ml_norm_g=P["ml_norm_g"][l].reshape(1, -1),
        gq=P["mla_q_norm_g"][l].reshape(1, -1), gkv=P["mla_kv_norm_g"][l].reshape(1, -1),
    )


def _rope_table(L):
    pos = jnp.arange(L, dtype=F32)
    inv = ROPE_BASE ** (-jnp.arange(0, MLA_ROPE, 2, dtype=F32) / MLA_ROPE)
    ang = pos[:, None] * inv[None, :]
    cos, sin = jnp.cos(ang), jnp.sin(ang)
    return jnp.concatenate([cos, cos, -sin, sin], axis=-1)


def _encoder(x, mods, P, W, wgu_bf, wd_bf):
    B, L, _ = x.shape
    T = B * L
    x2d = x.reshape(T, D_MODEL)
    plan = _FftPlan(L)
    tables = _fft_tables(plan)
    cs = _rope_table(L)
    depth = len(W)
    for l in range(depth):
        wl = W[l]
        sh1, sc1, g1, sh2, sc2, g2 = [mods[l][:, k].reshape(B, 1, D_MODEL) for k in range(6)]
        proj, side = _in_proj(x2d, wl["norm1_g"], sc1, sh1, wl["w_in_r"], wl["side_b"], L)
        hf, hb = _mlstm(proj, side, B, L)
        h2 = _hyena_features(L, P["hy_fw1"][l], P["hy_fb1"][l], P["hy_fw2"][l], P["hy_fb2"][l], P["hy_freq"][l])
        spectra = _hyena_spectra(plan, h2, P["hy_fw3"][l], P["hy_log_decay"][l], P["hy_skip"][l], tables[0], tables[2])
        z = _hyconv(plan, proj, COL_HY, proj, COL_HY + HY_WIDTH, wl["cw"], spectra, 0, tables, B, True)
        o_hy = _hyconv(plan, z, 0, proj, COL_HY + 2 * HY_WIDTH, wl["cw"], spectra, 1, tables, B, False)
        q, k, v = _mla_prep(proj, side, cs, wl["gq"], wl["gkv"], wl["wuq_ext"], wl["wuk"], wl["wuvt"], B, L)
        o_mla = _flash(q, k, v, B, L)
        merged = _merge(hf, hb, proj, wl["ml_norm_g"], o_hy, o_mla, wl["wml"], wl["why"], wl["wmla"], L)
        x2d = _out_proj(merged, wl["w_out"], x2d, g1, L)
        hp, idx, wgt = _route(x2d, wl["norm2_g"], sc2, sh2, P["w_router"], P["router_bias"], L)
        src, pos, tile_e, num_used = _moe_plan(idx, T)
        ys = _experts(hp, src, tile_e, num_used, wgu_bf, wd_bf, l)
        x2d = _combine(ys, pos, wgt, x2d, g2, P["final_g"].reshape(1, -1), L, final=(l == depth - 1))
    return x2d.reshape(B, L, D_MODEL)


def kernel(x_prompt, x_sample, c_prompt, c_sample, w_ada, b_ada, norm1_g, norm2_g, w_in, ml_gate_b, ml_norm_g,
           hy_conv_w, hy_conv_b, hy_fw1, hy_fb1, hy_fw2, hy_fb2, hy_fw3, hy_freq, hy_log_decay, hy_skip,
           mla_q_norm_g, mla_kv_norm_g, w_uq, w_ukv, w_br_ml, w_br_hy, w_br_mla, w_out, w_router, router_bias,
           w_gate_up, w_down, final_g):
    P = dict(w_ada=w_ada, b_ada=b_ada, norm1_g=norm1_g, norm2_g=norm2_g, w_in=w_in, ml_gate_b=ml_gate_b,
             ml_norm_g=ml_norm_g, hy_conv_w=hy_conv_w, hy_conv_b=hy_conv_b, hy_fw1=hy_fw1, hy_fb1=hy_fb1,
             hy_fw2=hy_fw2, hy_fb2=hy_fb2, hy_fw3=hy_fw3, hy_freq=hy_freq, hy_log_decay=hy_log_decay,
             hy_skip=hy_skip, mla_q_norm_g=mla_q_norm_g, mla_kv_norm_g=mla_kv_norm_g, w_uq=w_uq, w_ukv=w_ukv,
             w_br_ml=w_br_ml, w_br_hy=w_br_hy, w_br_mla=w_br_mla, w_out=w_out, w_router=w_router,
             router_bias=router_bias, w_gate_up=w_gate_up, w_down=w_down, final_g=final_g)
    depth = w_in.shape[0]
    W = [_prep_layer(P, l) for l in range(depth)]
    wgu_bf = w_gate_up.astype(BF16)
    wd_bf = w_down.astype(BF16)
    bp, bs = c_prompt.shape[0], c_sample.shape[0]
    rows = -(-(bp + bs) // SUBLANES) * SUBLANES
    c_all = jnp.zeros((rows, D_MODEL), F32).at[:bp].set(c_prompt).at[bp:bp + bs].set(c_sample)
    mods_p, mods_s = [], []
    for l in range(depth):
        mod = _ada(c_all, w_ada, b_ada, l).reshape(rows, 6, D_MODEL)
        mods_p.append(mod[:bp])
        mods_s.append(mod[bp:bp + bs])
    y_prompt = _encoder(x_prompt, mods_p, P, W, wgu_bf, wd_bf)
    y_sample = _encoder(x_sample, mods_s, P, W, wgu_bf, wd_bf)
    return (y_prompt, y_sample)
```

```python
import functools
import math

import jax
import jax.numpy as jnp
from jax import lax
from jax.experimental import pallas as pl
from jax.experimental.pallas import tpu as pltpu

F32 = jnp.float32
BF16 = jnp.bfloat16
U32 = jnp.uint32
I32 = jnp.int32

D_MODEL = 2048
ML_HEADS = 4
ML_HEAD_DIM = 256
ML_WIDTH = ML_HEADS * ML_HEAD_DIM
ML_CHUNK = 128
HY_WIDTH = 1024
HY_ORDER = 2
HY_BANDS = 16
HY_FFN = 64
MLA_HEADS = 8
MLA_Q_RANK = 512
MLA_KV_RANK = 256
MLA_NOPE = 128
MLA_ROPE = 64
MLA_V = 128
MLA_VE = MLA_V + 16
MLA_WIDTH = MLA_HEADS * MLA_V
MLA_QK = MLA_NOPE + MLA_ROPE
ROPE_BASE = 10000.0
N_EXPERTS = 16
N_GROUPS = 4
EXPERTS_PER_GROUP = N_EXPERTS // N_GROUPS
D_EXPERT = 1024
EPS = 1e-6

LANES = 128
SUBLANES = 8
VMEM_LIMIT = 56 * 1024 * 1024

COL_ML = 0
COL_HY = 4 * ML_WIDTH
COL_BRG = COL_HY + 3 * HY_WIDTH
COL_CQ = COL_BRG + 3 * D_MODEL
COL_CKV = COL_CQ + MLA_Q_RANK
COL_SIDE = COL_CKV + MLA_KV_RANK
SIDE_W = 256
SIDE_GATE = 128
IN_COLS_R = COL_SIDE + SIDE_W
IN_TN = 1024

FFT_N2 = 128
FFT_TPAD = SUBLANES
FFT_KPAD = SUBLANES
FFT_UNROLL_OUTER = 16
FFT_UNROLL_MID = 8
HY_CT = 2 * LANES

MOE_TILE = 512
GATHER_TILE = 256
ROUTE_TM = 1024
ATTN_TQ = 2048
ATTN_TK = 2048
ATTN_SPLIT = 2


def _cparams(*sem):
    return pltpu.CompilerParams(dimension_semantics=sem, vmem_limit_bytes=VMEM_LIMIT)


def _rms(x, g):
    return x * lax.rsqrt(jnp.mean(x * x, axis=-1, keepdims=True) + EPS) * g


def _log_sigmoid(x):
    return -(jnp.maximum(-x, 0.0) + jnp.log1p(jnp.exp(-jnp.abs(x))))


def _sigmoid(x):
    return 1.0 / (1.0 + jnp.exp(-x))


def _dot(a, b, **kw):
    return jnp.dot(a, b, preferred_element_type=F32, **kw)


def _dot_nt(a, b, **kw):
    return lax.dot_general(a, b, (((1,), (1,)), ((), ())), preferred_element_type=F32, **kw)


HI = lax.Precision.HIGHEST


def _ada_kernel(c_ref, w_ref, b_ref, o_ref):
    c = c_ref[...]
    s = c * _sigmoid(c)
    o_ref[...] = _dot(s.astype(BF16), w_ref[...].astype(BF16)) + b_ref[...]


def _ada(c_all, w_ada, b_ada, l):
    bp = c_all.shape[0]
    n = w_ada.shape[-1]
    tn = 1024
    return pl.pallas_call(
        _ada_kernel, name="ada_mod",
        grid=(n // tn,),
        in_specs=[
            pl.BlockSpec((bp, D_MODEL), lambda j: (0, 0)),
            pl.BlockSpec((None, D_MODEL, tn), lambda j: (l, 0, j)),
            pl.BlockSpec((None, 1, tn), lambda j: (l, 0, j)),
        ],
        out_specs=pl.BlockSpec((bp, tn), lambda j: (0, j)),
        out_shape=jax.ShapeDtypeStruct((bp, n), F32),
        compiler_params=_cparams("arbitrary"),
    )(c_all, w_ada, b_ada.reshape(b_ada.shape[0], 1, n))


def _in_kernel(x_ref, g_ref, sc_ref, sh_ref, w_ref, sb_ref, o_ref, side_ref, hn_ref, *, nj):
    j = pl.program_id(1)

    @pl.when(j == 0)
    def _():
        y = _rms(x_ref[...], g_ref[...])
        hn_ref[...] = (y * (1.0 + sc_ref[0]) + sh_ref[0]).astype(BF16)

    acc = _dot(hn_ref[...], w_ref[...])
    o_ref[...] = acc.astype(BF16)

    @pl.when(j == nj - 1)
    def _():
        side_ref[...] = acc[:, IN_TN - SIDE_W:] + sb_ref[...]


def _in_proj(x2d, g, sc, sh, w_in_r, side_b, L):
    T = x2d.shape[0]
    tm = min(1024, L)
    nj = IN_COLS_R // IN_TN
    bidx = lambda i, j: ((i * tm) // L, 0, 0)
    return pl.pallas_call(
        functools.partial(_in_kernel, nj=nj), name="in_proj",
        grid=(T // tm, nj),
        in_specs=[
            pl.BlockSpec((tm, D_MODEL), lambda i, j: (i, 0)),
            pl.BlockSpec((1, D_MODEL), lambda i, j: (0, 0)),
            pl.BlockSpec((1, 1, D_MODEL), bidx),
            pl.BlockSpec((1, 1, D_MODEL), bidx),
            pl.BlockSpec((D_MODEL, IN_TN), lambda i, j: (0, j)),
            pl.BlockSpec((1, SIDE_W), lambda i, j: (0, 0)),
        ],
        out_specs=[
            pl.BlockSpec((tm, IN_TN), lambda i, j: (i, j)),
            pl.BlockSpec((tm, SIDE_W), lambda i, j: (i, 0)),
        ],
        out_shape=[
            jax.ShapeDtypeStruct((T, IN_COLS_R), BF16),
            jax.ShapeDtypeStruct((T, SIDE_W), F32),
        ],
        scratch_shapes=[pltpu.VMEM((tm, D_MODEL), BF16)],
        compiler_params=_cparams("arbitrary", "arbitrary"),
    )(x2d, g, sc, sh, w_in_r, side_b)


def _mlstm_chunk(q, k, v, li_col, lf_col, b_col, li_row, b_row, mask, C_ref, n_ref, m_ref, idx):
    m = m_ref[idx]
    C = C_ref[idx]
    n = n_ref[idx]
    Dm = jnp.where(mask, b_col - b_row + li_row, -jnp.inf)
    inter = b_col + m
    m_row = jnp.maximum(inter, jnp.max(Dm, axis=-1, keepdims=True))
    w_inter = jnp.exp(inter - m_row)
    s = _dot_nt(q, k) * jnp.exp(Dm - m_row)
    num = w_inter * _dot(q, C.astype(BF16)) + _dot(s.astype(BF16), v)
    qn = jnp.sum(q.astype(F32) * n, axis=-1, keepdims=True)
    den = w_inter * qn + jnp.sum(s, axis=-1, keepdims=True)
    h = num / jnp.maximum(jnp.abs(den), jnp.exp(-m_row))
    bL = jnp.sum(lf_col, axis=0, keepdims=True)
    g = bL - b_col + li_col
    m_new = jnp.maximum(bL + m, jnp.max(g, axis=0, keepdims=True))
    a = jnp.exp(bL + m - m_new)
    kw = k.astype(F32) * jnp.exp(g - m_new)
    C_ref[idx] = a * C + _dot(kw.T.astype(BF16), v)
    n_ref[idx] = a * n + jnp.sum(kw, axis=0, keepdims=True)
    m_ref[idx] = m_new
    return h


def _mlstm_kernel(qf_ref, kf_ref, vf_ref, sf_ref, qb_ref, kb_ref, vb_ref, sb_ref,
                  hf_ref, hb_ref, C_ref, n_ref, m_ref):
    c = pl.program_id(1)

    @pl.when(c == 0)
    def _():
        C_ref[...] = jnp.zeros_like(C_ref)
        n_ref[...] = jnp.zeros_like(n_ref)
        m_ref[...] = jnp.zeros_like(m_ref)

    row = lax.broadcasted_iota(I32, (ML_CHUNK, ML_CHUNK), 0)
    col = lax.broadcasted_iota(I32, (ML_CHUNK, ML_CHUNK), 1)
    lower = (col <= row)
    upper = (col >= row)
    tri_l = lower.astype(F32)
    tri_u = upper.astype(F32)
    kscale = ML_HEAD_DIM ** -0.5

    for d, (q_ref, k_ref, v_ref, s_ref, o_ref) in enumerate(
            ((qf_ref, kf_ref, vf_ref, sf_ref, hf_ref), (qb_ref, kb_ref, vb_ref, sb_ref, hb_ref))):
        gsub = s_ref[:, SIDE_GATE:SIDE_GATE + LANES]
        gT = gsub.T
        lsig = _log_sigmoid(gsub)
        lsigT = _log_sigmoid(gT)
        if d == 0:
            b_all = _dot(tri_l, lsig, precision=HI)
            b_allT = _dot(lsigT, tri_u, precision=HI)
            mask = lower
        else:
            b_all = _dot(tri_u, lsig, precision=HI)
            b_allT = _dot(lsigT, tri_l, precision=HI)
            mask = upper
        for h in range(ML_HEADS):
            ci = (2 * d) * ML_HEADS + h
            cf = (2 * d + 1) * ML_HEADS + h
            hs = slice(h * ML_HEAD_DIM, (h + 1) * ML_HEAD_DIM)
            q = q_ref[:, hs]
            k = k_ref[:, hs] * kscale
            v = v_ref[:, hs]
            hout = _mlstm_chunk(
                q, k, v,
                gsub[:, ci:ci + 1], lsig[:, cf:cf + 1], b_all[:, cf:cf + 1],
                gT[ci:ci + 1, :], b_allT[cf:cf + 1, :], mask,
                C_ref, n_ref, m_ref, d * ML_HEADS + h)
            o_ref[:, hs] = hout.astype(BF16)


def _mlstm(proj, side, B, L):
    T = B * L
    nc = L // ML_CHUNK
    fwd = lambda cb: (lambda b, c: (b * nc + c, cb))
    bwd = lambda cb: (lambda b, c: (b * nc + nc - 1 - c, cb))
    blk = lambda im: pl.BlockSpec((ML_CHUNK, ML_WIDTH), im)
    sblk = lambda im: pl.BlockSpec((ML_CHUNK, SIDE_W), im)
    nst = 2 * ML_HEADS
    return pl.pallas_call(
        _mlstm_kernel, name="mlstm",
        grid=(B, nc),
        in_specs=[blk(fwd(0)), blk(fwd(1)), blk(fwd(2)), sblk(fwd(0)),
                  blk(bwd(0)), blk(bwd(1)), blk(bwd(2)), sblk(bwd(0))],
        out_specs=[blk(fwd(0)), blk(bwd(0))],
        out_shape=[jax.ShapeDtypeStruct((T, ML_WIDTH), BF16)] * 2,
        scratch_shapes=[pltpu.VMEM((nst, ML_HEAD_DIM, ML_HEAD_DIM), F32),
                        pltpu.VMEM((nst, 1, ML_HEAD_DIM), F32),
                        pltpu.VMEM((nst, 1, 1), F32)],
        compiler_params=_cparams("arbitrary", "arbitrary"),
    )(proj, proj, proj, side, proj, proj, proj, side)


class _FftPlan:
    def __init__(self, L):
        self.L = L
        self.N = 2 * L
        self.N2 = FFT_N2
        self.N1 = self.N // self.N2
        self.N1h = self.N1 // 2
        self.K1 = self.N1h + 1
        self.K1p = -(-self.K1 // SUBLANES) * SUBLANES
        self.PT = self.N2 + FFT_TPAD
        self.P2 = 2 * self.N2 + FFT_KPAD


def _fft_tables(p):
    k1 = jnp.arange(p.K1p, dtype=I32)
    n1 = jnp.arange(p.N1h, dtype=I32)
    n2 = jnp.arange(p.N2, dtype=I32)
    n = p.N2 * n1[None, :] + n2[:, None]
    ph = (k1[None, :, None] * n[:, None, :]) % p.N
    ang = ph.astype(F32) * (2.0 * math.pi / p.N)
    valid = (k1 < p.K1)[None, :, None]
    c = jnp.where(valid, jnp.cos(ang), 0.0)
    s = jnp.where(valid, jnp.sin(ang), 0.0)
    t1 = jnp.concatenate([c, -s], axis=1).astype(BF16)
    wk = jnp.where((k1 == 0) | (k1 == p.N1h), 1.0, 2.0) / p.N
    wk = jnp.where(k1 < p.K1, wk, 0.0)[None, None, :]
    ct = jnp.swapaxes(c, 1, 2) * wk
    st = jnp.swapaxes(s, 1, 2) * wk
    t3 = jnp.concatenate([ct, -st], axis=2).astype(BF16)
    a2 = ((n2[:, None] * n2[None, :]) % p.N2).astype(F32) * (2.0 * math.pi / p.N2)
    c2, s2 = jnp.cos(a2), jnp.sin(a2)
    g2f = jnp.block([[c2, s2], [-s2, c2]]).astype(BF16)
    g2i = jnp.block([[c2, -s2], [s2, c2]]).astype(BF16)
    return t1, t3, g2f, g2i


def _pack_bf16_pair(lo, hi):
    lb = lax.bitcast_convert_type(lo.astype(BF16).astype(F32), U32)
    hb = lax.bitcast_convert_type(hi.astype(BF16).astype(F32), U32)
    return hb | (lb >> 16)


def _unpack_bf16_pair(w):
    lo = lax.bitcast_convert_type(w << 16, F32)
    hi = lax.bitcast_convert_type(w & jnp.uint32(0xFFFF0000), F32)
    return lo, hi


def _pack_lanes(x):
    return _pack_bf16_pair(x[:, :LANES], x[:, LANES:])


def _unpack_lanes(w):
    lo, hi = _unpack_bf16_pair(w)
    return jnp.concatenate([lo.astype(BF16), hi.astype(BF16)], axis=1)


def _fft_stage1(p, tst_ref, kd_ref, t1_ref):
    def body(n2, carry):
        xs = _unpack_lanes(tst_ref[pl.ds(n2, p.N1h, stride=p.PT), :])
        w = _pack_lanes(_dot(t1_ref[n2], xs))
        kd_ref[pl.ds(n2, p.K1p, stride=p.P2), :] = w[:p.K1p]
        kd_ref[pl.ds(p.N2 + n2, p.K1p, stride=p.P2), :] = w[p.K1p:]
        return carry
    lax.fori_loop(0, p.N2, body, 0, unroll=FFT_UNROLL_OUTER)


def _hyfeat_kernel(w1t_ref, w1c_ref, w1s_ref, b1_ref, w2_ref, b2_ref, fr_ref, o_ref, *, L, rows):
    i = pl.program_id(0)
    t = (lax.broadcasted_iota(I32, (rows, 1), 0) + i * rows).astype(F32) / L
    bands = (lax.broadcasted_iota(I32, (1, LANES), 1) + 1).astype(F32)
    ang = (2.0 * math.pi * t) * bands
    z = (t * w1t_ref[...] + _dot(jnp.cos(ang), w1c_ref[...], precision=HI)
         + _dot(jnp.sin(ang), w1s_ref[...], precision=HI) + b1_ref[...])
    h = jnp.sin(fr_ref[0:1, :] * z)
    h = jnp.sin(fr_ref[1:2, :] * (_dot(h, w2_ref[...], precision=HI) + b2_ref[...]))
    o_ref[...] = h


def _hyena_features(L, fw1, fb1, fw2, fb2, freq):
    rows = min(512, L)
    w1c = jnp.zeros((LANES, HY_FFN), F32).at[:HY_BANDS].set(fw1[1:1 + HY_BANDS])
    w1s = jnp.zeros((LANES, HY_FFN), F32).at[:HY_BANDS].set(fw1[1 + HY_BANDS:])
    full = lambda shp: pl.BlockSpec(shp, lambda i: (0,) * len(shp))
    return pl.pallas_call(
        functools.partial(_hyfeat_kernel, L=L, rows=rows), name="hy_feat",
        grid=(L // rows,),
        in_specs=[full((1, HY_FFN)), full((LANES, HY_FFN)), full((LANES, HY_FFN)), full((1, HY_FFN)),
                  full((HY_FFN, HY_FFN)), full((1, HY_FFN)), full((2, HY_FFN))],
        out_specs=pl.BlockSpec((rows, HY_FFN), lambda i: (i, 0)),
        out_shape=jax.ShapeDtypeStruct((L, HY_FFN), F32),
        compiler_params=_cparams("arbitrary"),
    )(fw1[0:1], w1c, w1s, fb1.reshape(1, -1), fw2, fb2.reshape(1, -1), freq)


def _hyfilt_kernel(h2_ref, wf_ref, wb_ref, df_ref, db_ref, skip_ref, t1_ref, g2_ref, o_ref,
                   tst_ref, kd_ref, *, p):
    L, N2 = p.L, p.N2
    decay_f = jnp.exp(df_ref[...])
    decay_b = jnp.exp(db_ref[...])
    w_fb = jnp.concatenate([wf_ref[...], wb_ref[...]], axis=1)

    def gen(n1, carry):
        r0 = pl.multiple_of(n1 * N2, N2)
        t = (lax.broadcasted_iota(I32, (N2, 1), 0) + r0).astype(F32) / L
        h2 = h2_ref[pl.ds(r0, N2), :]
        fb = _dot(h2, w_fb, precision=HI)
        f = fb[:, :LANES] * jnp.exp(-t * decay_f)
        b = fb[:, LANES:] * jnp.exp(-t * decay_b)
        b = jnp.where(t == 0.0, 0.0, b)
        s0 = pl.multiple_of(n1 * p.PT, SUBLANES)
        tst_ref[pl.ds(s0, N2), :] = _pack_bf16_pair(f, b)
        return carry + jnp.sum(jnp.abs(f) + jnp.abs(b), axis=0, keepdims=True)

    l1 = lax.fori_loop(0, p.N1h, gen, jnp.zeros((1, LANES), F32), unroll=2)
    inv = 1.0 / l1
    _fft_stage1(p, tst_ref, kd_ref, t1_ref)

    def spec(k1, carry):
        s0 = pl.multiple_of(k1 * p.P2, SUBLANES)
        x = _dot(g2_ref[...], _unpack_lanes(kd_ref[pl.ds(s0, 2 * N2), :]))
        xf, xb = x[:, :LANES], x[:, LANES:]
        kr = (xf[:N2] + xb[:N2]) * inv + skip_ref[...]
        ki = (xf[N2:] - xb[N2:]) * inv
        o_ref[k1] = jnp.concatenate([kr, ki], axis=0).astype(BF16)
        return carry
    lax.fori_loop(0, p.K1, spec, 0, unroll=FFT_UNROLL_MID)


def _hyena_spectra(p, h2, fw3, log_decay, skip, t1, g2f):
    nct = HY_WIDTH // LANES
    col = lambda d: (lambda o, c: (0, (2 * o + d) * nct + c))
    full = lambda shp: pl.BlockSpec(shp, lambda o, c: (0,) * len(shp))
    ld = log_decay.reshape(1, -1)
    return pl.pallas_call(
        functools.partial(_hyfilt_kernel, p=p), name="hy_spectra",
        grid=(HY_ORDER, nct),
        in_specs=[full((p.L, HY_FFN)),
                  pl.BlockSpec((HY_FFN, LANES), col(0)), pl.BlockSpec((HY_FFN, LANES), col(1)),
                  pl.BlockSpec((1, LANES), col(0)), pl.BlockSpec((1, LANES), col(1)),
                  pl.BlockSpec((None, 1, LANES), lambda o, c: (o, 0, c)),
                  full(t1.shape), full(g2f.shape)],
        out_specs=pl.BlockSpec((None, p.K1, 2 * p.N2, LANES), lambda o, c: (o, 0, 0, c)),
        out_shape=jax.ShapeDtypeStruct((HY_ORDER, p.K1, 2 * p.N2, HY_WIDTH), BF16),
        scratch_shapes=[pltpu.VMEM((p.N1h * p.PT, LANES), U32), pltpu.VMEM((p.K1p * p.P2, LANES), U32)],
        compiler_params=_cparams("arbitrary", "arbitrary"),
    )(h2, fw3, fw3, ld, ld, skip.reshape(HY_ORDER, 1, HY_WIDTH), t1, g2f)


def _conv3_chunk(u_ref, cw_ref, n1, nchunks, L):
    N2 = FFT_N2
    lanes = u_ref.shape[1]
    pk = 2 * SUBLANES
    r0 = pl.multiple_of(n1 * N2, N2)
    cur = u_ref[pl.ds(r0, N2), :].astype(F32)
    pstart = pl.multiple_of(jnp.maximum(r0 - pk, 0), pk)
    nstart = pl.multiple_of(jnp.minimum(r0 + N2, L - pk), pk)
    prev = u_ref[pl.ds(pstart, pk), :].astype(F32)[pk - 1:pk, :]
    nxt = u_ref[pl.ds(nstart, pk), :].astype(F32)[0:1, :]
    prev = prev * jnp.where(n1 > 0, 1.0, 0.0)
    nxt = nxt * jnp.where(n1 < nchunks - 1, 1.0, 0.0)
    row = lax.broadcasted_iota(I32, (N2, lanes), 0)
    up = jnp.where(row == 0, prev, pltpu.roll(cur, 1, axis=0))
    dn = jnp.where(row == N2 - 1, nxt, pltpu.roll(cur, N2 - 1, axis=0))
    return cw_ref[0:1, :] * up + cw_ref[1:2, :] * cur + cw_ref[2:3, :] * dn + cw_ref[3:4, :]


def _hyconv_kernel(ua_ref, ub_ref, cwa_ref, cwb_ref, sp_ref, t1_ref, t3_ref, g2f_ref, g2i_ref,
                   o_ref, tst_ref, kd_ref, *, p, pre_a):
    L, N2 = p.L, p.N2

    def chunk_a(n1):
        if pre_a:
            return _conv3_chunk(ua_ref, cwa_ref, n1, p.N1h, L)
        return ua_ref[pl.ds(pl.multiple_of(n1 * N2, N2), N2), :].astype(F32)

    def load(n1, carry):
        tst_ref[pl.ds(pl.multiple_of(n1 * p.PT, SUBLANES), N2), :] = _pack_lanes(chunk_a(n1))
        return carry
    lax.fori_loop(0, p.N1h, load, 0, unroll=2)

    _fft_stage1(p, tst_ref, kd_ref, t1_ref)

    def mid(k1, carry):
        s0 = pl.multiple_of(k1 * p.P2, SUBLANES)
        x = _dot(g2f_ref[...], _unpack_lanes(kd_ref[pl.ds(s0, 2 * N2), :]))
        xr, xi = x[:N2], x[N2:]
        kk = sp_ref[k1].astype(F32)
        kr, ki = kk[:N2], kk[N2:]
        y = jnp.concatenate([xr * kr - xi * ki, xr * ki + xi * kr], axis=0).astype(BF16)
        kd_ref[pl.ds(s0, 2 * N2), :] = _pack_lanes(_dot(g2i_ref[...], y))
        return carry
    lax.fori_loop(0, p.K1, mid, 0, unroll=FFT_UNROLL_MID)

    def stage3(n2, carry):
        br = kd_ref[pl.ds(n2, p.K1p, stride=p.P2), :]
        bi = kd_ref[pl.ds(N2 + n2, p.K1p, stride=p.P2), :]
        bc = _unpack_lanes(jnp.concatenate([br, bi], axis=0))
        tst_ref[pl.ds(n2, p.N1h, stride=p.PT), :] = _pack_lanes(_dot(t3_ref[n2], bc))
        return carry
    lax.fori_loop(0, N2, stage3, 0, unroll=FFT_UNROLL_OUTER)

    def store(n1, carry):
        lo, hi = _unpack_bf16_pair(tst_ref[pl.ds(pl.multiple_of(n1 * p.PT, SUBLANES), N2), :])
        y = jnp.concatenate([lo, hi], axis=1)
        g = _conv3_chunk(ub_ref, cwb_ref, n1, p.N1h, L)
        o_ref[pl.ds(pl.multiple_of(n1 * N2, N2), N2), :] = (g * y).astype(BF16)
        return carry
    lax.fori_loop(0, p.N1h, store, 0, unroll=2)


def _hyconv(p, a_arr, a_col, b_arr, b_col, cw_pack, spectra, order, tables, B, pre_a):
    L = p.L
    T = B * L
    nct = HY_WIDTH // HY_CT
    t1, t3, g2f, g2i = tables
    seq_bytes = L * HY_CT * 2
    const_bytes = 2 * (t1.size + t3.size + p.K1 * 2 * p.N2 * HY_CT)
    scratch_bytes = 4 * LANES * (p.N1h * p.PT + p.K1p * p.P2)
    single = 6 * seq_bytes + 2 * const_bytes + scratch_bytes > VMEM_LIMIT
    mode = dict(pipeline_mode=pl.Buffered(1)) if single else {}
    full = lambda shp: pl.BlockSpec(shp, lambda c, b: (0,) * len(shp), **mode)
    acb, bcb = a_col // HY_CT, b_col // HY_CT
    cwa_cb = (a_col - COL_HY) // HY_CT if pre_a else 0
    cwb_cb = (b_col - COL_HY) // HY_CT
    return pl.pallas_call(
        functools.partial(_hyconv_kernel, p=p, pre_a=pre_a), name="hy_conv",
        grid=(nct, B),
        in_specs=[pl.BlockSpec((L, HY_CT), lambda c, b: (b, acb + c), **mode),
                  pl.BlockSpec((L, HY_CT), lambda c, b: (b, bcb + c), **mode),
                  pl.BlockSpec((SUBLANES, HY_CT), lambda c, b: (0, cwa_cb + c)),
                  pl.BlockSpec((SUBLANES, HY_CT), lambda c, b: (0, cwb_cb + c)),
                  pl.BlockSpec((None, p.K1, 2 * p.N2, HY_CT), lambda c, b: (order, 0, 0, c), **mode),
                  full(t1.shape), full(t3.shape), full(g2f.shape), full(g2i.shape)],
        out_specs=pl.BlockSpec((L, HY_CT), lambda c, b: (b, c)),
        out_shape=jax.ShapeDtypeStruct((T, HY_WIDTH), BF16),
        scratch_shapes=[pltpu.VMEM((p.N1h * p.PT, LANES), U32), pltpu.VMEM((p.K1p * p.P2, LANES), U32)],
        compiler_params=_cparams("arbitrary", "arbitrary"),
    )(a_arr, b_arr, cw_pack, cw_pack, spectra, t1, t3, g2f, g2i)


def _mlaprep_kernel(cq_ref, ckv_ref, side_ref, cs_ref, gq_ref, gkv_ref, wuq_ref, wuk_ref, wuvt_ref,
                    q_ref, k_ref, vt_ref):
    cqn = _rms(cq_ref[...].astype(F32), gq_ref[...]).astype(BF16)
    ckvn = _rms(ckv_ref[...].astype(F32), gkv_ref[...]).astype(BF16)
    qa = _dot(cqn, wuq_ref[...])
    kn = _dot(ckvn, wuk_ref[...])
    vt = _dot_nt(wuvt_ref[...], ckvn)
    cs = cs_ref[...]
    cos2, sin2 = cs[:, :MLA_ROPE], cs[:, MLA_ROPE:]
    side = side_ref[...]
    k_rope = (side[:, :MLA_ROPE] * cos2 + side[:, MLA_ROPE:2 * MLA_ROPE] * sin2).astype(BF16)
    scale = MLA_QK ** -0.5 * math.log2(math.e)
    hw = MLA_NOPE + 2 * MLA_ROPE
    extra = MLA_VE - MLA_V
    ones_rows = jnp.where(lax.broadcasted_iota(I32, (extra, vt.shape[1]), 0) == 0, 1.0, 0.0).astype(BF16)
    for h in range(MLA_HEADS):
        b0 = h * hw
        q_rope = qa[:, b0 + MLA_NOPE:b0 + MLA_QK] * cos2 + qa[:, b0 + MLA_QK:b0 + hw] * sin2
        q_ref[h, :, :MLA_NOPE] = (qa[:, b0:b0 + MLA_NOPE] * scale).astype(BF16)
        q_ref[h, :, MLA_NOPE:] = (q_rope * scale).astype(BF16)
        k_ref[h, :, :MLA_NOPE] = kn[:, h * MLA_NOPE:(h + 1) * MLA_NOPE].astype(BF16)
        k_ref[h, :, MLA_NOPE:] = k_rope
        vt_ref[h, :MLA_V, :] = vt[h * MLA_V:(h + 1) * MLA_V, :].astype(BF16)
        vt_ref[h, MLA_V:, :] = ones_rows


def _mla_prep(proj, side, cs, gq, gkv, wuq_ext, wuk, wuvt, B, L):
    T = B * L
    tm = min(512, L)
    nb = L // tm
    hw = MLA_NOPE + 2 * MLA_ROPE
    full = lambda shp: pl.BlockSpec(shp, lambda i: (0,) * len(shp))
    oidx = lambda i: (i // nb, 0, i % nb, 0)
    return pl.pallas_call(
        _mlaprep_kernel, name="mla_prep",
        grid=(T // tm,),
        in_specs=[pl.BlockSpec((tm, MLA_Q_RANK), lambda i: (i, COL_CQ // MLA_Q_RANK)),
                  pl.BlockSpec((tm, MLA_KV_RANK), lambda i: (i, COL_CKV // MLA_KV_RANK)),
                  pl.BlockSpec((tm, SIDE_W), lambda i: (i, 0)),
                  pl.BlockSpec((tm, 2 * MLA_ROPE), lambda i: (i % nb, 0)),
                  full((1, MLA_Q_RANK)), full((1, MLA_KV_RANK)),
                  full((MLA_Q_RANK, MLA_HEADS * hw)), full((MLA_KV_RANK, MLA_HEADS * MLA_NOPE)),
                  full((MLA_HEADS * MLA_V, MLA_KV_RANK))],
        out_specs=[pl.BlockSpec((None, MLA_HEADS, tm, MLA_QK), oidx),
                   pl.BlockSpec((None, MLA_HEADS, tm, MLA_QK), oidx),
                   pl.BlockSpec((None, MLA_HEADS, MLA_VE, tm), lambda i: (i // nb, 0, 0, i % nb))],
        out_shape=[jax.ShapeDtypeStruct((B, MLA_HEADS, L, MLA_QK), BF16),
                   jax.ShapeDtypeStruct((B, MLA_HEADS, L, MLA_QK), BF16),
                   jax.ShapeDtypeStruct((B, MLA_HEADS, MLA_VE, L), BF16)],
        compiler_params=_cparams("arbitrary"),
    )(proj, proj, side, cs, gq, gkv, wuq_ext, wuk, wuvt)


def _flash_kernel(q_ref, k_ref, vt_ref, o_ref, *, tq, tk, nsplit):
    L = k_ref.shape[0]
    sub = tq // nsplit
    qs = [q_ref[pl.ds(i * sub, sub), :] for i in range(nsplit)]

    units = [(j, i) for j in range(L // tk) for i in range(nsplit)]

    def scores(unit):
        j, i = unit
        return _dot_nt(k_ref[pl.ds(j * tk, tk), :], qs[i])

    m = [jnp.full((1, sub), -jnp.inf, F32) for _ in range(nsplit)]
    acc = [jnp.zeros((MLA_VE, sub), F32) for _ in range(nsplit)]
    st = scores(units[0])
    for n, (j, i) in enumerate(units):
        st_next = scores(units[n + 1]) if n + 1 < len(units) else None
        m_new = jnp.maximum(m[i], jnp.max(st, axis=0, keepdims=True))
        alpha = jnp.exp2(m[i] - m_new)
        pt = jnp.exp2(st - m_new).astype(BF16)
        acc[i] = alpha * acc[i] + _dot(vt_ref[:, pl.ds(j * tk, tk)], pt)
        m[i] = m_new
        st = st_next
    for i in range(nsplit):
        o_ref[pl.ds(i * sub, sub), :] = (acc[i][:MLA_V] / acc[i][MLA_V:MLA_V + 1]).T.astype(BF16)


def _flash(q, k, v, B, L):
    tq = min(ATTN_TQ, L)
    tk = min(ATTN_TK, L)
    nq = L // tq
    return pl.pallas_call(
        functools.partial(_flash_kernel, tq=tq, tk=tk, nsplit=ATTN_SPLIT), name="mla_attn",
        grid=(B, MLA_HEADS, nq),
        in_specs=[pl.BlockSpec((None, None, tq, MLA_QK), lambda b, h, i: (b, h, i, 0)),
                  pl.BlockSpec((None, None, L, MLA_QK), lambda b, h, i: (b, h, 0, 0)),
                  pl.BlockSpec((None, None, MLA_VE, L), lambda b, h, i: (b, h, 0, 0))],
        out_specs=pl.BlockSpec((tq, MLA_V), lambda b, h, i: (b * nq + i, h)),
        out_shape=jax.ShapeDtypeStruct((B * L, MLA_WIDTH), BF16),
        compiler_params=_cparams("arbitrary", "arbitrary", "arbitrary"),
    )(q, k, v)


def _merge_kernel(hf_ref, hb_ref, mlo_ref, mlg_ref, ohy_ref, omla_ref, gml_ref, ghy_ref, gmla_ref,
                  wml_ref, why_ref, wmla_ref, o_ref, oml_ref):
    @pl.when(pl.program_id(1) == 0)
    def _():
        for h in range(ML_HEADS):
            hs = slice(h * ML_HEAD_DIM, (h + 1) * ML_HEAD_DIM)
            x = hf_ref[:, hs].astype(F32) + hb_ref[:, hs].astype(F32)
            y = _rms(x, mlg_ref[:, hs])
            oml_ref[:, hs] = (y * _sigmoid(mlo_ref[:, hs].astype(F32))).astype(BF16)

    acc = _sigmoid(gml_ref[...].astype(F32)) * _dot(oml_ref[...], wml_ref[...])
    acc += _sigmoid(ghy_ref[...].astype(F32)) * _dot(ohy_ref[...], why_ref[...])
    acc += _sigmoid(gmla_ref[...].astype(F32)) * _dot(omla_ref[...], wmla_ref[...])
    o_ref[...] = acc.astype(BF16)


def _merge(hf, hb, proj, ml_norm_g, o_hy, o_mla, wml, why, wmla, L):
    T = hf.shape[0]
    tm = min(1024, L)
    tn = 512
    row = lambda cb: (lambda i, j: (i, cb))
    gate = lambda k: (lambda i, j: (i, (COL_BRG + k * D_MODEL) // tn + j))
    wsp = pl.BlockSpec((ML_WIDTH, tn), lambda i, j: (0, j))
    return pl.pallas_call(
        _merge_kernel, name="merge",
        grid=(T // tm, D_MODEL // tn),
        in_specs=[pl.BlockSpec((tm, ML_WIDTH), row(0)), pl.BlockSpec((tm, ML_WIDTH), row(0)),
                  pl.BlockSpec((tm, ML_WIDTH), row(3)),
                  pl.BlockSpec((1, ML_WIDTH), lambda i, j: (0, 0)),
                  pl.BlockSpec((tm, HY_WIDTH), row(0)), pl.BlockSpec((tm, MLA_WIDTH), row(0)),
                  pl.BlockSpec((tm, tn), gate(0)), pl.BlockSpec((tm, tn), gate(1)), pl.BlockSpec((tm, tn), gate(2)),
                  wsp, wsp, wsp],
        out_specs=pl.BlockSpec((tm, tn), lambda i, j: (i, j)),
        out_shape=jax.ShapeDtypeStruct((T, D_MODEL), BF16),
        scratch_shapes=[pltpu.VMEM((tm, ML_WIDTH), BF16)],
        compiler_params=_cparams("arbitrary", "arbitrary"),
    )(hf, hb, proj, ml_norm_g, o_hy, o_mla, proj, proj, proj, wml, why, wmla)


def _outproj_kernel(m_ref, w_ref, x_ref, g_ref, o_ref):
    o_ref[...] = x_ref[...] + g_ref[0] * _dot(m_ref[...], w_ref[...])


def _out_proj(merged, w_out, x2d, g1, L):
    T = x2d.shape[0]
    tm = min(1024, L)
    tn = 512
    return pl.pallas_call(
        _outproj_kernel, name="out_proj",
        grid=(T // tm, D_MODEL // tn),
        in_specs=[pl.BlockSpec((tm, D_MODEL), lambda i, j: (i, 0)),
                  pl.BlockSpec((D_MODEL, tn), lambda i, j: (0, j)),
                  pl.BlockSpec((tm, tn), lambda i, j: (i, j)),
                  pl.BlockSpec((1, 1, tn), lambda i, j: ((i * tm) // L, 0, j))],
        out_specs=pl.BlockSpec((tm, tn), lambda i, j: (i, j)),
        out_shape=jax.ShapeDtypeStruct((T, D_MODEL), F32),
        compiler_params=_cparams("arbitrary", "arbitrary"),
    )(merged, w_out, x2d, g1)


def _route_kernel(x_ref, g_ref, sc_ref, sh_ref, wr_ref, rb_ref, hp_ref, idx_ref, wgt_ref, lg_ref):
    half = D_MODEL // 2
    nsb = x_ref.shape[0] // LANES
    hn = _rms(x_ref[...], g_ref[...]) * (1.0 + sc_ref[0]) + sh_ref[0]
    hp_ref[...] = _pack_bf16_pair(hn[:, :half], hn[:, half:])
    for sb in range(nsb):
        logits = _dot_nt(wr_ref[...], hn[sb * LANES:(sb + 1) * LANES, :], precision=HI)
        lg_ref[pl.ds(sb * N_EXPERTS, N_EXPERTS), :] = _sigmoid(logits)
    srow = [lg_ref[pl.ds(e, nsb, stride=N_EXPERTS), :] for e in range(N_EXPERTS)]
    rows = [srow[e] + rb_ref[e:e + 1, :] for e in range(N_EXPERTS)]
    gs = []
    for g in range(N_GROUPS):
        r = rows[g * EXPERTS_PER_GROUP:(g + 1) * EXPERTS_PER_GROUP]
        best = None
        for a in range(EXPERTS_PER_GROUP):
            for b in range(a + 1, EXPERTS_PER_GROUP):
                s = r[a] + r[b]
                best = s if best is None else jnp.maximum(best, s)
        gs.append(best)
    gsel = jnp.zeros_like(gs[0], dtype=I32)
    gbest = gs[0]
    for g in range(1, N_GROUPS):
        better = gs[g] > gbest
        gsel = jnp.where(better, g, gsel)
        gbest = jnp.where(better, gs[g], gbest)
    masked = [jnp.where(gsel == (e // EXPERTS_PER_GROUP), rows[e], -jnp.inf) for e in range(N_EXPERTS)]
    picks = []
    for _ in range(2):
        bi = jnp.zeros_like(gsel)
        bv = masked[0]
        bs = srow[0]
        for e in range(1, N_EXPERTS):
            better = masked[e] > bv
            bi = jnp.where(better, e, bi)
            bv = jnp.where(better, masked[e], bv)
            bs = jnp.where(better, srow[e], bs)
        picks.append((bi, bs))
        masked = [jnp.where(bi == e, -jnp.inf, masked[e]) for e in range(N_EXPERTS)]
    (i1, s1), (i2, s2) = picks
    tot = s1 + s2
    idx_ref[0] = i1
    idx_ref[1] = i2
    wgt_ref[0] = s1 / tot
    wgt_ref[1] = s2 / tot


def _route(x2d, g, sc, sh, w_router, router_bias, L):
    T = x2d.shape[0]
    tm = min(ROUTE_TM, L)
    nsb = tm // LANES
    bidx = lambda i: ((i * tm) // L, 0, 0)
    rb = jnp.broadcast_to(router_bias.reshape(N_EXPERTS, 1), (N_EXPERTS, LANES))
    hp, idx, wgt = _route_call(x2d, g, sc, sh, w_router, rb, T, tm, nsb, bidx)
    return hp, idx.reshape(2, T), wgt.reshape(2, T)


def _route_call(x2d, g, sc, sh, w_router, rb, T, tm, nsb, bidx):
    return pl.pallas_call(
        _route_kernel, name="route",
        grid=(T // tm,),
        in_specs=[pl.BlockSpec((tm, D_MODEL), lambda i: (i, 0)),
                  pl.BlockSpec((1, D_MODEL), lambda i: (0, 0)),
                  pl.BlockSpec((1, 1, D_MODEL), bidx), pl.BlockSpec((1, 1, D_MODEL), bidx),
                  pl.BlockSpec((N_EXPERTS, D_MODEL), lambda i: (0, 0)),
                  pl.BlockSpec((N_EXPERTS, LANES), lambda i: (0, 0))],
        out_specs=[pl.BlockSpec((tm, D_MODEL // 2), lambda i: (i, 0)),
                   pl.BlockSpec((2, nsb, LANES), lambda i: (0, i, 0)),
                   pl.BlockSpec((2, nsb, LANES), lambda i: (0, i, 0))],
        out_shape=[jax.ShapeDtypeStruct((T, D_MODEL // 2), U32),
                   jax.ShapeDtypeStruct((2, T // LANES, LANES), I32),
                   jax.ShapeDtypeStruct((2, T // LANES, LANES), F32)],
        scratch_shapes=[pltpu.VMEM((nsb * N_EXPERTS, LANES), F32)],
        compiler_params=_cparams("arbitrary"),
    )(x2d, g, sc, sh, w_router.T, rb)


def _moe_plan(idx, T):
    A = 2 * T
    e = idx.reshape(A)
    tok = jnp.tile(jnp.arange(T, dtype=I32), 2)
    onehot = (e[:, None] == jnp.arange(N_EXPERTS, dtype=I32)[None, :]).astype(I32)
    csum = jnp.cumsum(onehot, axis=0)
    counts = csum[-1]
    rank = jnp.sum(csum * onehot, axis=1) - 1
    padded = ((counts + MOE_TILE - 1) // MOE_TILE) * MOE_TILE
    ends = jnp.cumsum(padded)
    starts = ends - padded
    pos = jnp.sum(starts[None, :] * onehot, axis=1) + rank
    R = A + N_EXPERTS * MOE_TILE
    src = jnp.zeros((R,), I32).at[pos].set(tok, unique_indices=True)
    n_tiles = R // MOE_TILE
    tile_start = jnp.arange(n_tiles, dtype=I32) * MOE_TILE
    tile_e = jnp.minimum(jnp.sum((tile_start[:, None] >= ends[None, :]).astype(I32), axis=1), N_EXPERTS - 1)
    num_used = (ends[-1] // MOE_TILE).astype(I32).reshape(1)
    return src, pos.reshape(2, T).astype(I32), tile_e.astype(I32), num_used


def _row_gather_copy(src_hbm, dst_ref, sem, src_row, dst_row):
    return pltpu.make_async_copy(src_hbm.at[pl.ds(src_row, 1)], dst_ref.at[pl.ds(dst_row, 1)], sem)


def _issue_row_gathers(src_hbm, idx_ref, dst_ref, sem, rows, dst_base=0):
    for r in range(rows):
        _row_gather_copy(src_hbm, dst_ref, sem, idx_ref[0, 0, r], dst_base + r).start()


def _wait_row_gathers(src_hbm, dst_ref, sem):
    pltpu.make_async_copy(src_hbm.at[pl.ds(0, dst_ref.shape[0])], dst_ref, sem).wait()


def _expert_kernel(te_ref, nt_ref, idx_ref, idxn_ref, hp_hbm, wgu_ref, wd_ref, o_ref, xbuf_ref, sems):
    t = pl.program_id(0)
    nt = nt_ref[0]
    slot = lax.rem(t, 2)
    half = D_MODEL // 2

    @pl.when(t == 0)
    def _():
        _issue_row_gathers(hp_hbm, idx_ref, xbuf_ref.at[0], sems.at[0], MOE_TILE)

    @pl.when(t < nt)
    def _():
        _wait_row_gathers(hp_hbm, xbuf_ref.at[slot], sems.at[slot])
        _issue_row_gathers(hp_hbm, idxn_ref, xbuf_ref.at[1 - slot], sems.at[1 - slot], MOE_TILE)
        lo, hi = _unpack_bf16_pair(xbuf_ref[slot])
        h1 = _dot(lo.astype(BF16), wgu_ref[:half, :]) + _dot(hi.astype(BF16), wgu_ref[half:, :])
        a, b = h1[:, :D_EXPERT], h1[:, D_EXPERT:]
        act = (a * _sigmoid(a) * b).astype(BF16)
        y = _dot(act, wd_ref[...])
        o_ref[...] = _pack_bf16_pair(y[:, :half], y[:, half:])

    @pl.when(t == nt - 1)
    def _():
        _wait_row_gathers(hp_hbm, xbuf_ref.at[1 - slot], sems.at[1 - slot])

    @pl.when(t >= nt)
    def _():
        o_ref[...] = jnp.zeros_like(o_ref)


def _experts(hp, src, tile_e, num_used, wgu, wd, l):
    R = src.shape[0]
    nt = R // MOE_TILE
    half = D_MODEL // 2
    src3 = src.reshape(nt, 1, MOE_TILE)
    grid_spec = pltpu.PrefetchScalarGridSpec(
        num_scalar_prefetch=2,
        grid=(nt,),
        in_specs=[pl.BlockSpec((1, 1, MOE_TILE), lambda t, te, n: (t, 0, 0), memory_space=pltpu.SMEM),
                  pl.BlockSpec((1, 1, MOE_TILE), lambda t, te, n: (jnp.minimum(t + 1, n[0] - 1), 0, 0),
                               memory_space=pltpu.SMEM),
                  pl.BlockSpec(memory_space=pl.ANY),
                  pl.BlockSpec((None, None, D_MODEL, 2 * D_EXPERT), lambda t, te, n: (l, te[t], 0, 0)),
                  pl.BlockSpec((None, None, D_EXPERT, D_MODEL), lambda t, te, n: (l, te[t], 0, 0))],
        out_specs=pl.BlockSpec((MOE_TILE, half), lambda t, te, n: (t, 0)),
        scratch_shapes=[pltpu.VMEM((2, MOE_TILE, half), U32), pltpu.SemaphoreType.DMA((2,))],
    )
    return pl.pallas_call(
        _expert_kernel, name="moe_experts",
        grid_spec=grid_spec,
        out_shape=jax.ShapeDtypeStruct((R, half), U32),
        compiler_params=_cparams("arbitrary"),
    )(tile_e, num_used, src3, src3, hp, wgu, wd)


def _combine_kernel(p1_ref, p2_ref, p1n_ref, p2n_ref, ys_hbm, w_ref, x_ref, g_ref, fg_ref, o_ref,
                    buf0_ref, buf1_ref, sems, *, rows, final):
    i = pl.program_id(0)
    n = pl.num_programs(0)
    half = D_MODEL // 2

    @pl.when(i == 0)
    def _():
        _issue_row_gathers(ys_hbm, p1_ref, buf0_ref, sems.at[0], rows)
        _issue_row_gathers(ys_hbm, p2_ref, buf0_ref, sems.at[0], rows, rows)

    def run(cur_ref, nxt_ref, cur_sem, nxt_sem):
        _wait_row_gathers(ys_hbm, cur_ref, cur_sem)
        _issue_row_gathers(ys_hbm, p1n_ref, nxt_ref, nxt_sem, rows)
        _issue_row_gathers(ys_hbm, p2n_ref, nxt_ref, nxt_sem, rows, rows)

        w = w_ref[...]
        w1, w2 = w[:, 0:1], w[:, 1:2]
        lo1, hi1 = _unpack_bf16_pair(cur_ref[pl.ds(0, rows), :])
        lo2, hi2 = _unpack_bf16_pair(cur_ref[pl.ds(rows, rows), :])
        g = g_ref[0]
        xlo = x_ref[:, :half] + g[:, :half] * (w1 * lo1 + w2 * lo2)
        xhi = x_ref[:, half:] + g[:, half:] * (w1 * hi1 + w2 * hi2)
        if final:
            ms = (jnp.sum(xlo * xlo, axis=-1, keepdims=True)
                  + jnp.sum(xhi * xhi, axis=-1, keepdims=True)) / D_MODEL
            r = lax.rsqrt(ms + EPS)
            xlo = xlo * r * fg_ref[:, :half]
            xhi = xhi * r * fg_ref[:, half:]
        o_ref[:, :half] = xlo
        o_ref[:, half:] = xhi

        @pl.when(i == n - 1)
        def _():
            _wait_row_gathers(ys_hbm, nxt_ref, nxt_sem)

    even = lax.rem(i, 2) == 0

    @pl.when(even)
    def _():
        run(buf0_ref, buf1_ref, sems.at[0], sems.at[1])

    @pl.when(jnp.logical_not(even))
    def _():
        run(buf1_ref, buf0_ref, sems.at[1], sems.at[0])


def _combine(ys, pos, wgt, x2d, g2, final_g, L, final):
    T = x2d.shape[0]
    rows = GATHER_TILE
    nt = T // rows
    half = D_MODEL // 2
    cur = lambda: pl.BlockSpec((1, 1, rows), lambda i: (i, 0, 0), memory_space=pltpu.SMEM)
    nxt = lambda: pl.BlockSpec((1, 1, rows), lambda i: (jnp.minimum(i + 1, nt - 1), 0, 0), memory_space=pltpu.SMEM)
    p1 = pos[0].reshape(nt, 1, rows)
    p2 = pos[1].reshape(nt, 1, rows)
    return pl.pallas_call(
        functools.partial(_combine_kernel, rows=rows, final=final), name="moe_combine",
        grid=(nt,),
        in_specs=[cur(), cur(), nxt(), nxt(),
                  pl.BlockSpec(memory_space=pl.ANY),
                  pl.BlockSpec((rows, 2), lambda i: (i, 0)),
                  pl.BlockSpec((rows, D_MODEL), lambda i: (i, 0)),
                  pl.BlockSpec((1, 1, D_MODEL), lambda i: ((i * rows) // L, 0, 0)),
                  pl.BlockSpec((1, D_MODEL), lambda i: (0, 0))],
        out_specs=pl.BlockSpec((rows, D_MODEL), lambda i: (i, 0)),
        out_shape=jax.ShapeDtypeStruct((T, D_MODEL), F32),
        scratch_shapes=[pltpu.VMEM((2 * rows, half), U32), pltpu.VMEM((2 * rows, half), U32),
                        pltpu.SemaphoreType.DMA((2,))],
        compiler_params=_cparams("arbitrary"),
    )(p1, p2, p1, p2, ys, wgt.T, x2d, g2, final_g)


def _swap_halves(w):
    h = w.shape[-1] // 2
    return jnp.concatenate([w[..., h:], w[..., :h]], axis=-1)


def _prep_layer(P, l):
    w = P["w_in"][l]
    o_g = 4 * ML_WIDTH
    o_hy = o_g + 4 * ML_HEADS
    o_cq = o_hy + 3 * HY_WIDTH
    o_ckv = o_cq + MLA_Q_RANK
    o_kr = o_ckv + MLA_KV_RANK
    o_br = o_kr + MLA_ROPE
    kr = w[:, o_kr:o_br]
    side_pad = jnp.zeros((D_MODEL, SIDE_W - SIDE_GATE - 4 * ML_HEADS), F32)
    w_in_r = jnp.concatenate(
        [w[:, :o_g], w[:, o_hy:o_cq], w[:, o_br:], w[:, o_cq:o_ckv], w[:, o_ckv:o_kr],
         kr, _swap_halves(kr), w[:, o_g:o_hy], side_pad], axis=1).astype(BF16)
    side_b = jnp.zeros((1, SIDE_W), F32).at[0, SIDE_GATE:SIDE_GATE + 4 * ML_HEADS].set(P["ml_gate_b"][l].reshape(-1))
    wuq = P["w_uq"][l].reshape(MLA_Q_RANK, MLA_HEADS, MLA_QK)
    qr = wuq[..., MLA_NOPE:]
    wuq_ext = jnp.concatenate([wuq, _swap_halves(qr)], axis=-1).reshape(MLA_Q_RANK, -1).astype(BF16)
    wukv = P["w_ukv"][l].reshape(MLA_KV_RANK, MLA_HEADS, MLA_NOPE + MLA_V)
    cw = jnp.concatenate([P["hy_conv_w"][l], P["hy_conv_b"][l][None, :],
                          jnp.zeros((SUBLANES - 4, 3 * HY_WIDTH), F32)], axis=0)
    return dict(
        w_in_r=w_in_r, side_b=side_b, wuq_ext=wuq_ext, cw=cw,
        wuk=wukv[..., :MLA_NOPE].reshape(MLA_KV_RANK, -1).astype(BF16),
        wuvt=wukv[..., MLA_NOPE:].reshape(MLA_KV_RANK, -1).T.astype(BF16),
        wml=P["w_br_ml"][l].astype(BF16), why=P["w_br_hy"][l].astype(BF16), wmla=P["w_br_mla"][l].astype(BF16),
        w_out=P["w_out"][l].astype(BF16),
        norm1_g=P["norm1_g"][l].reshape(1, -1), norm2_g=P["norm2_g"][l].reshape(1, -1),
        ml_norm_g=P["ml_norm_g"][l].reshape(1, -1),
        gq=P["mla_q_norm_g"][l].reshape(1, -1), gkv=P["mla_kv_norm_g"][l].reshape(1, -1),
    )


def _rope_table(L):
    pos = jnp.arange(L, dtype=F32)
    inv = ROPE_BASE ** (-jnp.arange(0, MLA_ROPE, 2, dtype=F32) / MLA_ROPE)
    ang = pos[:, None] * inv[None, :]
    cos, sin = jnp.cos(ang), jnp.sin(ang)
    return jnp.concatenate([cos, cos, -sin, sin], axis=-1)


def _encoder(x, mods, P, W, wgu_bf, wd_bf):
    B, L, _ = x.shape
    T = B * L
    x2d = x.reshape(T, D_MODEL)
    plan = _FftPlan(L)
    tables = _fft_tables(plan)
    cs = _rope_table(L)
    depth = len(W)
    for l in range(depth):
        wl = W[l]
        sh1, sc1, g1, sh2, sc2, g2 = [mods[l][:, k].reshape(B, 1, D_MODEL) for k in range(6)]
        proj, side = _in_proj(x2d, wl["norm1_g"], sc1, sh1, wl["w_in_r"], wl["side_b"], L)
        hf, hb = _mlstm(proj, side, B, L)
        h2 = _hyena_features(L, P["hy_fw1"][l], P["hy_fb1"][l], P["hy_fw2"][l], P["hy_fb2"][l], P["hy_freq"][l])
        spectra = _hyena_spectra(plan, h2, P["hy_fw3"][l], P["hy_log_decay"][l], P["hy_skip"][l], tables[0], tables[2])
        z = _hyconv(plan, proj, COL_HY, proj, COL_HY + HY_WIDTH, wl["cw"], spectra, 0, tables, B, True)
        o_hy = _hyconv(plan, z, 0, proj, COL_HY + 2 * HY_WIDTH, wl["cw"], spectra, 1, tables, B, False)
        q, k, v = _mla_prep(proj, side, cs, wl["gq"], wl["gkv"], wl["wuq_ext"], wl["wuk"], wl["wuvt"], B, L)
        o_mla = _flash(q, k, v, B, L)
        merged = _merge(hf, hb, proj, wl["ml_norm_g"], o_hy, o_mla, wl["wml"], wl["why"], wl["wmla"], L)
        x2d = _out_proj(merged, wl["w_out"], x2d, g1, L)
        hp, idx, wgt = _route(x2d, wl["norm2_g"], sc2, sh2, P["w_router"], P["router_bias"], L)
        src, pos, tile_e, num_used = _moe_plan(idx, T)
        ys = _experts(hp, src, tile_e, num_used, wgu_bf, wd_bf, l)
        x2d = _combine(ys, pos, wgt, x2d, g2, P["final_g"].reshape(1, -1), L, final=(l == depth - 1))
    return x2d.reshape(B, L, D_MODEL)


def kernel(x_prompt, x_sample, c_prompt, c_sample, w_ada, b_ada, norm1_g, norm2_g, w_in, ml_gate_b, ml_norm_g,
           hy_conv_w, hy_conv_b, hy_fw1, hy_fb1, hy_fw2, hy_fb2, hy_fw3, hy_freq, hy_log_decay, hy_skip,
           mla_q_norm_g, mla_kv_norm_g, w_uq, w_ukv, w_br_ml, w_br_hy, w_br_mla, w_out, w_router, router_bias,
           w_gate_up, w_down, final_g):
    P = dict(w_ada=w_ada, b_ada=b_ada, norm1_g=norm1_g, norm2_g=norm2_g, w_in=w_in, ml_gate_b=ml_gate_b,
             ml_norm_g=ml_norm_g, hy_conv_w=hy_conv_w, hy_conv_b=hy_conv_b, hy_fw1=hy_fw1, hy_fb1=hy_fb1,
             hy_fw2=hy_fw2, hy_fb2=hy_fb2, hy_fw3=hy_fw3, hy_freq=hy_freq, hy_log_decay=hy_log_decay,
             hy_skip=hy_skip, mla_q_norm_g=mla_q_norm_g, mla_kv_norm_g=mla_kv_norm_g, w_uq=w_uq, w_ukv=w_ukv,
             w_br_ml=w_br_ml, w_br_hy=w_br_hy, w_br_mla=w_br_mla, w_out=w_out, w_router=w_router,
             router_bias=router_bias, w_gate_up=w_gate_up, w_down=w_down, final_g=final_g)
    depth = w_in.shape[0]
    W = [_prep_layer(P, l) for l in range(depth)]
    wgu_bf = w_gate_up.astype(BF16)
    wd_bf = w_down.astype(BF16)
    bp, bs = c_prompt.shape[0], c_sample.shape[0]
    rows = -(-(bp + bs) // SUBLANES) * SUBLANES
    c_all = jnp.zeros((rows, D_MODEL), F32).at[:bp].set(c_prompt).at[bp:bp + bs].set(c_sample)
    mods_p, mods_s = [], []
    for l in range(depth):
        mod = _ada(c_all, w_ada, b_ada, l).reshape(rows, 6, D_MODEL)
        mods_p.append(mod[:bp])
        mods_s.append(mod[bp:bp + bs])
    y_prompt = _encoder(x_prompt, mods_p, P, W, wgu_bf, wd_bf)
    y_sample = _encoder(x_sample, mods_s, P, W, wgu_bf, wd_bf)
    return (y_prompt, y_sample)
```
